```python
import math
import jax
import jax.numpy as jnp
from jax import lax
import numpy as np

D_MODEL = 1024
BATCH = 2
SEQ = 8192
DEPTH = 4
DEC_BATCH = 128
DEC_SEQ = 4
PAST_LEN = 8192
PAGE_SIZE = 128

N_MLA_LAYERS = (DEPTH + 1) // 2
N_NSA_LAYERS = DEPTH // 2
N_DENSE_LAYERS = (DEPTH + 1) // 2
N_MOE_LAYERS = DEPTH // 2

MLA_HEADS = 16
MLA_Q_RANK = 384
MLA_KV_RANK = 256
MLA_NOPE = 128
MLA_ROPE = 64
MLA_VDIM = 128
MLA_SCALE = (MLA_NOPE + MLA_ROPE) ** -0.5
ROPE_THETA = 10000.0

NSA_HEADS = 16
NSA_GROUPS = 2
NSA_HPG = NSA_HEADS // NSA_GROUPS
NSA_DH = 64
NSA_SCALE = NSA_DH ** -0.5
CMP_LEN = 32
CMP_STRIDE = 16
SEL_BLOCK = 64
N_SEL = 16
WINDOW = 512

N_BUCKETS = 32
MAX_DISTANCE = 128

N_MEM = 256
MEM_HEADS = 4
MEM_DH = 128
MEM_SCALE = MEM_DH ** -0.5

D_FF = 2816
N_EXPERTS = 8
TOP_K = 2
D_FF_EXPERT = 2816

ALPHA = (2 * DEPTH) ** 0.25
BETA = (8 * DEPTH) ** -0.25
LN_EPS = 1e-5
RMS_EPS = 1e-6
Q_BLOCK = 128

kernel_name = 'hybrid_mla_nsa_deepnorm_decode_step'


def layer_norm(x, g, b):
    xf = x.astype(jnp.float32)
    mu = jnp.mean(xf, -1, keepdims=True)
    var = jnp.mean(jnp.square(xf - mu), -1, keepdims=True)
    return ((xf - mu) * lax.rsqrt(var + LN_EPS) * g + b).astype(x.dtype)


def rms_norm(x, g):
    xf = x.astype(jnp.float32)
    return (xf * lax.rsqrt(jnp.mean(jnp.square(xf), -1, keepdims=True) + RMS_EPS) * g).astype(x.dtype)


def rope(x, pos):
    half = x.shape[-1] // 2
    inv = ROPE_THETA ** (-jnp.arange(half, dtype=jnp.float32) / half)
    ang = pos.astype(jnp.float32)[:, None] * inv[None, :]
    cos = jnp.cos(ang)[None, :, None, :]
    sin = jnp.sin(ang)[None, :, None, :]
    xf = x.astype(jnp.float32)
    x1, x2 = xf[..., :half], xf[..., half:]
    return jnp.concatenate([x1 * cos - x2 * sin, x2 * cos + x1 * sin], -1).astype(x.dtype)


def masked_softmax(s, mask):
    s = jnp.where(mask, s.astype(jnp.float32), -jnp.inf)
    m = jnp.max(s, -1, keepdims=True)
    m = jnp.where(jnp.isfinite(m), m, 0.0)
    e = jnp.exp(s - m)
    d = jnp.sum(e, -1, keepdims=True)
    return e / jnp.where(d > 0, d, 1.0)


def t5_bucket(rel):
    n = jnp.maximum(rel, 0)
    exact = N_BUCKETS // 2
    large = exact + (jnp.log(jnp.maximum(n, 1).astype(jnp.float32) / exact)
                     / math.log(MAX_DISTANCE / exact) * (N_BUCKETS - exact)).astype(jnp.int32)
    return jnp.where(n < exact, n, jnp.minimum(large, N_BUCKETS - 1))


def over_query_blocks(fn, n_q):
    out = lax.map(fn, jnp.arange(n_q // Q_BLOCK, dtype=jnp.int32) * Q_BLOCK)
    out = jnp.moveaxis(out, 0, 1)
    return out.reshape(out.shape[0], n_q, *out.shape[3:])


def gather_pages(pool, li, page_table):
    g = pool[li, page_table]
    return g.reshape(g.shape[0], -1, *pool.shape[3:])


def mla_project(x, pos, w_down, q_norm, kv_norm, w_uq, w_uk):
    B, S, _ = x.shape
    down = x @ w_down
    c_q = rms_norm(down[..., :MLA_Q_RANK], q_norm)
    c_kv = rms_norm(down[..., MLA_Q_RANK:MLA_Q_RANK + MLA_KV_RANK], kv_norm)
    k_pe = rope(down[..., None, MLA_Q_RANK + MLA_KV_RANK:], pos)[:, :, 0]
    q = (c_q @ w_uq).reshape(B, S, MLA_HEADS, MLA_NOPE + MLA_ROPE)
    q_pe = rope(q[..., MLA_NOPE:], pos)
    q_lat = jnp.einsum('bshn,rhn->bshr', q[..., :MLA_NOPE], w_uk)
    return q_lat, q_pe, c_kv, k_pe


def mla_attend(q_lat, q_pe, q_pos, c_kv, k_pe, k_pos, w_uv):
    s = (jnp.einsum('bshr,btr->bhst', q_lat, c_kv) + jnp.einsum('bshe,bte->bhst', q_pe, k_pe)).astype(jnp.float32) * MLA_SCALE
    p = masked_softmax(s, k_pos[None, :] <= q_pos[:, None]).astype(c_kv.dtype)
    o_lat = jnp.einsum('bhst,btr->bshr', p, c_kv)
    return jnp.einsum('bshr,rhv->bshv', o_lat, w_uv)


def mla_prompt(x, w_down, q_norm, kv_norm, w_uq, w_uk, w_uv, w_o):
    B, T, _ = x.shape
    pos = jnp.arange(T, dtype=jnp.int32)
    q_lat, q_pe, c_kv, k_pe = mla_project(x, pos, w_down, q_norm, kv_norm, w_uq, w_uk)

    def block(q0):
        return mla_attend(lax.dynamic_slice_in_dim(q_lat, q0, Q_BLOCK, 1),
                          lax.dynamic_slice_in_dim(q_pe, q0, Q_BLOCK, 1),
                          q0 + jnp.arange(Q_BLOCK, dtype=jnp.int32), c_kv, k_pe, pos, w_uv)

    o = over_query_blocks(block, T)
    return o.reshape(B, T, -1) @ w_o, c_kv, k_pe


def mla_sample(x, cache_ckv, cache_kpe, li, page_table, w_down, q_norm, kv_norm, w_uq, w_uk, w_uv, w_o):
    DB, S, _ = x.shape
    pos = PAST_LEN + jnp.arange(S, dtype=jnp.int32)
    q_lat, q_pe, c_kv, k_pe = mla_project(x, pos, w_down, q_norm, kv_norm, w_uq, w_uk)
    c_all = jnp.concatenate([gather_pages(cache_ckv, li, page_table), c_kv], 1)
    k_all = jnp.concatenate([gather_pages(cache_kpe, li, page_table), k_pe], 1)
    k_pos = jnp.arange(PAST_LEN + S, dtype=jnp.int32)
    o = mla_attend(q_lat, q_pe, pos, c_all, k_all, k_pos, w_uv)
    return o.reshape(DB, S, -1) @ w_o, c_kv, k_pe


def nsa_project(x, w_in):
    B, S, _ = x.shape
    h = x @ w_in
    qw = NSA_HEADS * NSA_DH
    kvw = NSA_GROUPS * NSA_DH
    q = h[..., :qw].reshape(B, S, NSA_GROUPS, NSA_HPG, NSA_DH)
    kv = h[..., qw:qw + 6 * kvw].reshape(B, S, 6, NSA_GROUPS, NSA_DH)
    gates = jax.nn.sigmoid(h[..., qw + 6 * kvw:].astype(jnp.float32)).reshape(B, S, 3, NSA_GROUPS, NSA_HPG)
    return q, kv, gates


def compress_rows(rows, pe, w1, w2):
    B, T = rows.shape[:2]
    nh = T // CMP_STRIDE
    halves = rows[:, :nh * CMP_STRIDE].reshape(B, nh, CMP_STRIDE, NSA_GROUPS, NSA_DH)
    first = jnp.einsum('bnrgd,rde->bnge', halves, w1[:CMP_STRIDE])
    second = jnp.einsum('bnrgd,rde->bnge', halves, w1[CMP_STRIDE:])
    const = jnp.einsum('rd,rde->e', pe, w1)
    return jax.nn.silu(first[:, :-1] + second[:, 1:] + const) @ w2


def selection_scores(p_grp, n_sel_blocks):
    nc = p_grp.shape[-1]
    ratio = SEL_BLOCK // CMP_STRIDE
    lead = CMP_LEN // CMP_STRIDE - 1
    span = ratio + lead
    total = ratio * n_sel_blocks + span
    P = jnp.pad(p_grp, [(0, 0)] * (p_grp.ndim - 1) + [(lead, total - lead - nc)])
    return sum(P[..., o:o + ratio * n_sel_blocks:ratio] for o in range(span))


def nsa_attend(q, gates, q_pos, kc_blk, vc_blk, sel_gather, n_sel_blocks, kw, vw, kw_pos, table):
    B, S = q.shape[:2]
    nc = kc_blk.shape[1]
    table_g = table.reshape(N_BUCKETS, NSA_GROUPS, NSA_HPG)
    c_end = jnp.arange(nc, dtype=jnp.int32) * CMP_STRIDE + (CMP_LEN - 1)
    rel_c = q_pos[:, None] - c_end[None, :]
    bias_c = table_g[t5_bucket(rel_c)].transpose(0, 2, 3, 1)
    s_c = jnp.einsum('bsgjd,bcgd->bsgjc', q, kc_blk).astype(jnp.float32) * NSA_SCALE + bias_c
    p_c = masked_softmax(s_c, (rel_c >= 0)[:, None, None, :])
    o_c = jnp.einsum('bsgjc,bcgd->bsgjd', p_c.astype(vc_blk.dtype), vc_blk)
    imp = selection_scores(jnp.sum(p_c, axis=3), n_sel_blocks)
    blk = jnp.arange(n_sel_blocks, dtype=jnp.int32)[None, :]
    cur = (q_pos // SEL_BLOCK)[:, None]
    forced = (blk == 0) | (blk == cur) | (blk == cur - 1)
    imp = jnp.where(forced[:, None, :], jnp.inf, jnp.where((blk <= cur)[:, None, :], imp, -jnp.inf))
    _, idx = lax.top_k(imp, min(N_SEL, n_sel_blocks))
    n = idx.shape[-1]
    ks, vs = sel_gather(idx)
    rel_s = q_pos[None, :, None, None, None] - (idx[..., None] * SEL_BLOCK + jnp.arange(SEL_BLOCK, dtype=jnp.int32))
    gi = jnp.arange(NSA_GROUPS)[None, None, :, None, None]
    bias_s = jnp.moveaxis(table_g[t5_bucket(rel_s), gi], -1, 3).reshape(B, S, NSA_GROUPS, NSA_HPG, n * SEL_BLOCK)
    s_s = jnp.einsum('bsgjd,bsgnrd->bsgjnr', q, ks).reshape(B, S, NSA_GROUPS, NSA_HPG, n * SEL_BLOCK)
    s_s = s_s.astype(jnp.float32) * NSA_SCALE + bias_s
    p_s = masked_softmax(s_s, (rel_s >= 0).reshape(B, S, NSA_GROUPS, 1, n * SEL_BLOCK))
    o_s = jnp.einsum('bsgjk,bsgkd->bsgjd', p_s.astype(vs.dtype), vs.reshape(B, S, NSA_GROUPS, n * SEL_BLOCK, NSA_DH))
    rel_w = q_pos[:, None] - kw_pos[None, :]
    bias_w = table_g[t5_bucket(rel_w)].transpose(0, 2, 3, 1)
    s_w = jnp.einsum('bsgjd,bkgd->bsgjk', q, kw).astype(jnp.float32) * NSA_SCALE + bias_w
    mask_w = (rel_w >= 0) & (rel_w < WINDOW) & (kw_pos >= 0)[None, :]
    p_w = masked_softmax(s_w, mask_w[:, None, None, :])
    o_w = jnp.einsum('bsgjk,bkgd->bsgjd', p_w.astype(vw.dtype), vw)
    o = (gates[:, :, 0, :, :, None] * o_c + gates[:, :, 1, :, :, None] * o_s
         + gates[:, :, 2, :, :, None] * o_w)
    return o.astype(q.dtype).reshape(B, S, NSA_HEADS * NSA_DH)


def nsa_prompt(x, w_in, pe_k, w1_k, w2_k, pe_v, w1_v, w2_v, w_o, table):
    B, T, _ = x.shape
    q, kv, gates = nsa_project(x, w_in)
    kc, vc, ks, vs, kw, vw = (kv[:, :, j] for j in range(6))
    kc_blk = compress_rows(kc, pe_k, w1_k, w2_k)
    vc_blk = compress_rows(vc, pe_v, w1_v, w2_v)
    ns = T // SEL_BLOCK
    ks_b = ks.reshape(B, ns, SEL_BLOCK, NSA_GROUPS, NSA_DH)
    vs_b = vs.reshape(B, ns, SEL_BLOCK, NSA_GROUPS, NSA_DH)
    bi = jnp.arange(B)[:, None, None, None]
    gi = jnp.arange(NSA_GROUPS)[None, None, :, None]

    def sel_gather(idx):
        return ks_b[bi, idx, :, gi, :], vs_b[bi, idx, :, gi, :]

    wpad = ((0, 0), (WINDOW, 0), (0, 0), (0, 0))
    kw_pad = jnp.pad(kw, wpad)
    vw_pad = jnp.pad(vw, wpad)

    def block(q0):
        kar = jnp.arange(WINDOW + Q_BLOCK, dtype=jnp.int32)
        return nsa_attend(lax.dynamic_slice_in_dim(q, q0, Q_BLOCK, 1),
                          lax.dynamic_slice_in_dim(gates, q0, Q_BLOCK, 1),
                          q0 + jnp.arange(Q_BLOCK, dtype=jnp.int32),
                          kc_blk, vc_blk, sel_gather, ns,
                          lax.dynamic_slice_in_dim(kw_pad, q0, WINDOW + Q_BLOCK, 1),
                          lax.dynamic_slice_in_dim(vw_pad, q0, WINDOW + Q_BLOCK, 1),
                          q0 - WINDOW + kar, table)

    o = over_query_blocks(block, T)
    keep = min(WINDOW, T)
    return o @ w_o, kc, vc, ks, vs, kw[:, T - keep:], vw[:, T - keep:]


def nsa_sample(x, li, cache_kc, cache_vc, cache_ks, cache_vs, state_kw, state_vw, page_table,
               w_in, pe_k, w1_k, w2_k, pe_v, w1_v, w2_v, w_o, table):
    DB, S, _ = x.shape
    q_pos = PAST_LEN + jnp.arange(S, dtype=jnp.int32)
    q, kv, gates = nsa_project(x, w_in)
    kc, vc, ks, vs, kw, vw = (kv[:, :, j] for j in range(6))
    kc_blk = compress_rows(jnp.concatenate([gather_pages(cache_kc, li, page_table), kc], 1), pe_k, w1_k, w2_k)
    vc_blk = compress_rows(jnp.concatenate([gather_pages(cache_vc, li, page_table), vc], 1), pe_v, w1_v, w2_v)
    j0 = PAST_LEN // SEL_BLOCK
    nb_new = -(-S // SEL_BLOCK)
    spad = ((0, 0), (0, nb_new * SEL_BLOCK - S), (0, 0), (0, 0))
    ks_new = jnp.pad(ks, spad).reshape(DB, nb_new, SEL_BLOCK, NSA_GROUPS, NSA_DH)
    vs_new = jnp.pad(vs, spad).reshape(DB, nb_new, SEL_BLOCK, NSA_GROUPS, NSA_DH)
    sub = PAGE_SIZE // SEL_BLOCK
    n_pages = page_table.shape[1]
    bi = jnp.arange(DB)[:, None, None, None]
    gi = jnp.arange(NSA_GROUPS)[None, None, :, None]

    def sel_gather(idx):
        phys = page_table[bi, jnp.minimum(idx // sub, n_pages - 1)][..., None]
        rows = (idx % sub)[..., None] * SEL_BLOCK + jnp.arange(SEL_BLOCK, dtype=jnp.int32)
        g5 = gi[..., None]
        jn = jnp.clip(idx - j0, 0, nb_new - 1)
        is_new = (idx >= j0)[..., None, None]
        k = jnp.where(is_new, ks_new[bi, jn, :, gi, :], cache_ks[li, phys, rows, g5])
        v = jnp.where(is_new, vs_new[bi, jn, :, gi, :], cache_vs[li, phys, rows, g5])
        return k, v

    win_buf = state_kw.shape[2]
    kw_all = jnp.concatenate([state_kw[li], kw], 1)
    vw_all = jnp.concatenate([state_vw[li], vw], 1)
    kw_pos = PAST_LEN - win_buf + jnp.arange(win_buf + S, dtype=jnp.int32)
    o = nsa_attend(q, gates, q_pos, kc_blk, vc_blk, sel_gather, j0 + nb_new, kw_all, vw_all, kw_pos, table)
    return o @ w_o, kc, vc, ks, vs, kw_all[:, S:], vw_all[:, S:]


def mem_kv(mem, w_kv):
    B, M, _ = mem.shape
    kv = (mem @ w_kv).reshape(B, M, 2, MEM_HEADS, MEM_DH)
    return kv[:, :, 0], kv[:, :, 1]


def mem_attend(x, mk, mv, w_q, w_o):
    B, S, _ = x.shape
    q = (x @ w_q).reshape(B, S, MEM_HEADS, MEM_DH)
    s = jnp.einsum('bshd,bmhd->bhsm', q, mk).astype(jnp.float32) * MEM_SCALE
    p = jax.nn.softmax(s, axis=-1).astype(mv.dtype)
    o = jnp.einsum('bhsm,bmhd->bshd', p, mv)
    return o.reshape(B, S, -1) @ w_o


def swiglu(x, w_gu, w_down):
    h = x @ w_gu
    f = h.shape[-1] // 2
    return (jax.nn.silu(h[..., :f]) * h[..., f:]) @ w_down


def moe_swiglu(x, router, w_gu, w_down):
    logits = (x @ router).astype(jnp.float32)
    top_v, top_i = lax.top_k(logits, TOP_K)
    w = jax.nn.softmax(top_v, axis=-1)
    combine = jnp.sum(jax.nn.one_hot(top_i, N_EXPERTS, dtype=jnp.float32) * w[..., None], axis=-2)
    y = jnp.zeros_like(x)
    for e in range(N_EXPERTS):
        y = y + combine[..., e:e + 1].astype(x.dtype) * swiglu(x, w_gu[e], w_down[e])
    return y


def setup_inputs(seed: int = 0) -> dict:
    key = jax.random.key(seed)
    keys = iter(jax.random.split(key, 64))

    def nrm(shape, scale=1.0):
        return jax.random.normal(next(keys), shape, jnp.float32) * scale

    n_pages = PAST_LEN // PAGE_SIZE
    used = DEC_BATCH * n_pages
    pool = used + max(1, used // 4)
    win_buf = min(WINDOW, PAST_LEN)
    page_table = jax.random.permutation(next(keys), pool)[:used].reshape(DEC_BATCH, n_pages).astype(jnp.int32)
    d = D_MODEL
    nsa_in = NSA_HEADS * NSA_DH + 6 * NSA_GROUPS * NSA_DH + 3 * NSA_HEADS
    return {
        'x_prompt': nrm((BATCH, SEQ, d)),
        'x_sample': nrm((DEC_BATCH, DEC_SEQ, d)),
        'cache_mla_ckv': nrm((N_MLA_LAYERS, pool, PAGE_SIZE, MLA_KV_RANK)),
        'cache_mla_kpe': nrm((N_MLA_LAYERS, pool, PAGE_SIZE, MLA_ROPE)),
        'cache_nsa_kc': nrm((N_NSA_LAYERS, pool, PAGE_SIZE, NSA_GROUPS, NSA_DH)),
        'cache_nsa_vc': nrm((N_NSA_LAYERS, pool, PAGE_SIZE, NSA_GROUPS, NSA_DH)),
        'cache_nsa_ks': nrm((N_NSA_LAYERS, pool, PAGE_SIZE, NSA_GROUPS, NSA_DH)),
        'cache_nsa_vs': nrm((N_NSA_LAYERS, pool, PAGE_SIZE, NSA_GROUPS, NSA_DH)),
        'state_nsa_kw': nrm((N_NSA_LAYERS, DEC_BATCH, win_buf, NSA_GROUPS, NSA_DH)),
        'state_nsa_vw': nrm((N_NSA_LAYERS, DEC_BATCH, win_buf, NSA_GROUPS, NSA_DH)),
        'cache_mem_k': nrm((DEPTH, DEC_BATCH, N_MEM, MEM_HEADS, MEM_DH)),
        'cache_mem_v': nrm((DEPTH, DEC_BATCH, N_MEM, MEM_HEADS, MEM_DH)),
        'page_table': page_table,
        'mem_prompt': nrm((BATCH, N_MEM, d)),
        'ln_g': 1.0 + nrm((DEPTH, 3, d), 0.01),
        'ln_b': nrm((DEPTH, 3, d), 0.01),
        'mla_w_down': nrm((N_MLA_LAYERS, d, MLA_Q_RANK + MLA_KV_RANK + MLA_ROPE), d ** -0.5),
        'mla_q_norm': 1.0 + nrm((N_MLA_LAYERS, MLA_Q_RANK), 0.01),
        'mla_kv_norm': 1.0 + nrm((N_MLA_LAYERS, MLA_KV_RANK), 0.01),
        'mla_w_uq': nrm((N_MLA_LAYERS, MLA_Q_RANK, MLA_HEADS * (MLA_NOPE + MLA_ROPE)), MLA_Q_RANK ** -0.5),
        'mla_w_uk': nrm((N_MLA_LAYERS, MLA_KV_RANK, MLA_HEADS, MLA_NOPE), MLA_KV_RANK ** -0.5),
        'mla_w_uv': nrm((N_MLA_LAYERS, MLA_KV_RANK, MLA_HEADS, MLA_VDIM), MLA_KV_RANK ** -0.5),
        'mla_w_o': nrm((N_MLA_LAYERS, MLA_HEADS * MLA_VDIM, d), BETA * (MLA_HEADS * MLA_VDIM) ** -0.5),
        'nsa_w_in': nrm((N_NSA_LAYERS, d, nsa_in), d ** -0.5),
        'nsa_cmp_pe_k': nrm((N_NSA_LAYERS, CMP_LEN, NSA_DH), 0.1),
        'nsa_cmp_w1_k': nrm((N_NSA_LAYERS, CMP_LEN, NSA_DH, NSA_DH), (CMP_LEN * NSA_DH) ** -0.5),
        'nsa_cmp_w2_k': nrm((N_NSA_LAYERS, NSA_DH, NSA_DH), NSA_DH ** -0.5),
        'nsa_cmp_pe_v': nrm((N_NSA_LAYERS, CMP_LEN, NSA_DH), 0.1),
        'nsa_cmp_w1_v': nrm((N_NSA_LAYERS, CMP_LEN, NSA_DH, NSA_DH), (CMP_LEN * NSA_DH) ** -0.5),
        'nsa_cmp_w2_v': nrm((N_NSA_LAYERS, NSA_DH, NSA_DH), NSA_DH ** -0.5),
        'nsa_w_o': nrm((N_NSA_LAYERS, NSA_HEADS * NSA_DH, d), BETA * (NSA_HEADS * NSA_DH) ** -0.5),
        'rel_bias': nrm((N_BUCKETS, NSA_HEADS), 0.3),
        'xa_w_q': nrm((DEPTH, d, MEM_HEADS * MEM_DH), d ** -0.5),
        'xa_w_kv': nrm((DEPTH, d, 2 * MEM_HEADS * MEM_DH), d ** -0.5),
        'xa_w_o': nrm((DEPTH, MEM_HEADS * MEM_DH, d), BETA * (MEM_HEADS * MEM_DH) ** -0.5),
        'ffn_w_gu': nrm((N_DENSE_LAYERS, d, 2 * D_FF), d ** -0.5),
        'ffn_w_down': nrm((N_DENSE_LAYERS, D_FF, d), BETA * D_FF ** -0.5),
        'moe_router': nrm((N_MOE_LAYERS, d, N_EXPERTS), d ** -0.5),
        'moe_w_gu': nrm((N_MOE_LAYERS, N_EXPERTS, d, 2 * D_FF_EXPERT), d ** -0.5),
        'moe_w_down': nrm((N_MOE_LAYERS, N_EXPERTS, D_FF_EXPERT, d), BETA * D_FF_EXPERT ** -0.5),
    }


def reference(x_prompt, x_sample, cache_mla_ckv, cache_mla_kpe, cache_nsa_kc, cache_nsa_vc,
              cache_nsa_ks, cache_nsa_vs, state_nsa_kw, state_nsa_vw, cache_mem_k, cache_mem_v,
              page_table, mem_prompt, ln_g, ln_b,
              mla_w_down, mla_q_norm, mla_kv_norm, mla_w_uq, mla_w_uk, mla_w_uv, mla_w_o,
              nsa_w_in, nsa_cmp_pe_k, nsa_cmp_w1_k, nsa_cmp_w2_k, nsa_cmp_pe_v, nsa_cmp_w1_v,
              nsa_cmp_w2_v, nsa_w_o, rel_bias, xa_w_q, xa_w_kv, xa_w_o,
              ffn_w_gu, ffn_w_down, moe_router, moe_w_gu, moe_w_down):
    xp, xs = x_prompt, x_sample
    p_ckv, p_kpe, p_kc, p_vc, p_ks, p_vs, p_kw, p_vw, p_mk, p_mv = ([] for _ in range(10))
    s_ckv, s_kpe, s_kc, s_vc, s_ks, s_vs, s_kw, s_vw = ([] for _ in range(8))
    for i in range(DEPTH):
        li = i // 2
        if i % 2 == 0:
            mla_w = (mla_w_down[li], mla_q_norm[li], mla_kv_norm[li], mla_w_uq[li],
                     mla_w_uk[li], mla_w_uv[li], mla_w_o[li])
            hp, ckv, kpe = mla_prompt(xp, *mla_w)
            p_ckv.append(ckv)
            p_kpe.append(kpe)
            hs, ckv, kpe = mla_sample(xs, cache_mla_ckv, cache_mla_kpe, li, page_table, *mla_w)
            s_ckv.append(ckv)
            s_kpe.append(kpe)
        else:
            nsa_w = (nsa_w_in[li], nsa_cmp_pe_k[li], nsa_cmp_w1_k[li], nsa_cmp_w2_k[li],
                     nsa_cmp_pe_v[li], nsa_cmp_w1_v[li], nsa_cmp_w2_v[li], nsa_w_o[li], rel_bias)
            hp, kc, vc, ks, vs, kw, vw = nsa_prompt(xp, *nsa_w)
            p_kc.append(kc); p_vc.append(vc); p_ks.append(ks); p_vs.append(vs)
            p_kw.append(kw); p_vw.append(vw)
            hs, kc, vc, ks, vs, kw, vw = nsa_sample(xs, li, cache_nsa_kc, cache_nsa_vc, cache_nsa_ks,
                                                    cache_nsa_vs, state_nsa_kw, state_nsa_vw,
                                                    page_table, *nsa_w)
            s_kc.append(kc); s_vc.append(vc); s_ks.append(ks); s_vs.append(vs)
            s_kw.append(kw); s_vw.append(vw)
        xp = layer_norm(ALPHA * xp + hp, ln_g[i, 0], ln_b[i, 0])
        xs = layer_norm(ALPHA * xs + hs, ln_g[i, 0], ln_b[i, 0])
        mk, mv = mem_kv(mem_prompt, xa_w_kv[i])
        p_mk.append(mk)
        p_mv.append(mv)
        xp = layer_norm(ALPHA * xp + mem_attend(xp, mk, mv, xa_w_q[i], xa_w_o[i]), ln_g[i, 1], ln_b[i, 1])
        xs = layer_norm(ALPHA * xs + mem_attend(xs, cache_mem_k[i], cache_mem_v[i], xa_w_q[i], xa_w_o[i]),
                        ln_g[i, 1], ln_b[i, 1])
        if i % 2 == 0:
            fp = swiglu(xp, ffn_w_gu[li], ffn_w_down[li])
            fs = swiglu(xs, ffn_w_gu[li], ffn_w_down[li])
        else:
            fp = moe_swiglu(xp, moe_router[li], moe_w_gu[li], moe_w_down[li])
            fs = moe_swiglu(xs, moe_router[li], moe_w_gu[li], moe_w_down[li])
        xp = layer_norm(ALPHA * xp + fp, ln_g[i, 2], ln_b[i, 2])
        xs = layer_norm(ALPHA * xs + fs, ln_g[i, 2], ln_b[i, 2])
    new_mla_ckv_prompt = jnp.stack(p_ckv)
    new_mla_kpe_prompt = jnp.stack(p_kpe)
    new_nsa_kc_prompt = jnp.stack(p_kc)
    new_nsa_vc_prompt = jnp.stack(p_vc)
    new_nsa_ks_prompt = jnp.stack(p_ks)
    new_nsa_vs_prompt = jnp.stack(p_vs)
    new_nsa_kw_prompt = jnp.stack(p_kw)
    new_nsa_vw_prompt = jnp.stack(p_vw)
    new_mem_k_prompt = jnp.stack(p_mk)
    new_mem_v_prompt = jnp.stack(p_mv)
    new_mla_ckv_sample = jnp.stack(s_ckv)
    new_mla_kpe_sample = jnp.stack(s_kpe)
    new_nsa_kc_sample = jnp.stack(s_kc)
    new_nsa_vc_sample = jnp.stack(s_vc)
    new_nsa_ks_sample = jnp.stack(s_ks)
    new_nsa_vs_sample = jnp.stack(s_vs)
    new_nsa_kw_sample = jnp.stack(s_kw)
    new_nsa_vw_sample = jnp.stack(s_vw)
    return (xp, xs, new_mla_ckv_prompt, new_mla_kpe_prompt, new_nsa_kc_prompt, new_nsa_vc_prompt,
            new_nsa_ks_prompt, new_nsa_vs_prompt, new_nsa_kw_prompt, new_nsa_vw_prompt,
            new_mem_k_prompt, new_mem_v_prompt, new_mla_ckv_sample, new_mla_kpe_sample,
            new_nsa_kc_sample, new_nsa_vc_sample, new_nsa_ks_sample, new_nsa_vs_sample,
            new_nsa_kw_sample, new_nsa_vw_sample)
```

```python
import functools
import math

import jax
import jax.numpy as jnp
import numpy as np
from jax import lax
from jax.experimental import pallas as pl
from jax.experimental.pallas import tpu as pltpu

F32 = jnp.float32
BF16 = jnp.bfloat16

DEPTH = 4
PAGE_SIZE = 128
MLA_HEADS = 16
MLA_Q_RANK = 384
MLA_KV_RANK = 256
MLA_NOPE = 128
MLA_ROPE = 64
MLA_VDIM = 128
MLA_SCALE = (MLA_NOPE + MLA_ROPE) ** -0.5
ROPE_THETA = 10000.0
NSA_HEADS = 16
NSA_GROUPS = 2
NSA_HPG = NSA_HEADS // NSA_GROUPS
NSA_DH = 64
NSA_SCALE = NSA_DH ** -0.5
CMP_LEN = 32
CMP_STRIDE = 16
SEL_BLOCK = 64
N_SEL = 16
WINDOW = 512
N_BUCKETS = 32
MAX_DISTANCE = 128
MEM_HEADS = 4
MEM_DH = 128
MEM_SCALE = MEM_DH ** -0.5
N_EXPERTS = 8
ALPHA = (2 * DEPTH) ** 0.25
LN_EPS = 1e-5
RMS_EPS = 1e-6
Q_BLOCK = 128

LANES = 128
VMEM_LIMIT = 56 * 1024 * 1024
ROW_TILE = 512
NEG_INF = float("-inf")


def _cparams(*sem):
    return pltpu.CompilerParams(dimension_semantics=tuple(sem), vmem_limit_bytes=VMEM_LIMIT)


def _dot(a, b):
    return jnp.dot(a, b, preferred_element_type=F32)


def _dot_nt(a, b):
    return lax.dot_general(a, b, (((1,), (1,)), ((), ())), preferred_element_type=F32)


def _layer_norm(h, g, b):
    mu = jnp.mean(h, -1, keepdims=True)
    d = h - mu
    var = jnp.mean(d * d, -1, keepdims=True)
    return d * lax.rsqrt(var + LN_EPS) * g + b


def _rms_norm(x, g):
    return x * lax.rsqrt(jnp.mean(x * x, -1, keepdims=True) + RMS_EPS) * g


def _mm_kernel(x_ref, w_ref, o_ref, *, scale):
    acc = _dot(x_ref[...].astype(BF16), w_ref[...])
    if scale is not None:
        acc = acc * scale
    o_ref[...] = acc.astype(o_ref.dtype)


def _mm(x, w, out_dtype=F32, scale=None, tm=ROW_TILE, name="mm"):
    M, K = x.shape
    N = w.shape[1]
    tm = math.gcd(M, tm)
    return pl.pallas_call(
        functools.partial(_mm_kernel, scale=scale),
        grid=(M // tm,),
        in_specs=[pl.BlockSpec((tm, K), lambda i: (i, 0)),
                  pl.BlockSpec((K, N), lambda i: (0, 0))],
        out_specs=pl.BlockSpec((tm, N), lambda i: (i, 0)),
        out_shape=jax.ShapeDtypeStruct((M, N), out_dtype),
        compiler_params=_cparams("parallel"),
        name=name,
    )(x, w)


def _mm_ln_kernel(a_ref, w_ref, res_ref, g_ref, b_ref, o_ref):
    acc = _dot(a_ref[...].astype(BF16), w_ref[...])
    o_ref[...] = _layer_norm(ALPHA * res_ref[...] + acc, g_ref[...], b_ref[...])


def _mm_ln(a, w, res, g, b, tm=ROW_TILE, name="mm_ln"):
    M, K = a.shape
    D = w.shape[1]
    tm = math.gcd(M, tm)
    return pl.pallas_call(
        _mm_ln_kernel,
        grid=(M // tm,),
        in_specs=[pl.BlockSpec((tm, K), lambda i: (i, 0)),
                  pl.BlockSpec((K, D), lambda i: (0, 0)),
                  pl.BlockSpec((tm, D), lambda i: (i, 0)),
                  pl.BlockSpec((1, D), lambda i: (0, 0)),
                  pl.BlockSpec((1, D), lambda i: (0, 0))],
        out_specs=pl.BlockSpec((tm, D), lambda i: (i, 0)),
        out_shape=jax.ShapeDtypeStruct((M, D), F32),
        compiler_params=_cparams("parallel"),
        name=name,
    )(a, w, res, g.reshape(1, D), b.reshape(1, D))


def _router_kernel(x_ref, r_ref, o_ref):
    logits = jnp.dot(x_ref[...], r_ref[...], preferred_element_type=F32,
                     precision=lax.Precision.HIGHEST)
    lane = lax.broadcasted_iota(jnp.int32, logits.shape, 1)
    logits = jnp.where(lane < N_EXPERTS, logits, NEG_INF)
    v1 = jnp.max(logits, -1, keepdims=True)
    i1 = jnp.min(jnp.where(logits == v1, lane, LANES), -1, keepdims=True)
    rest = jnp.where(lane == i1, NEG_INF, logits)
    v2 = jnp.max(rest, -1, keepdims=True)
    i2 = jnp.min(jnp.where(rest == v2, lane, LANES), -1, keepdims=True)
    e2 = jnp.exp(v2 - v1)
    w1 = 1.0 / (1.0 + e2)
    w2 = e2 / (1.0 + e2)
    o_ref[...] = jnp.where(lane == i1, w1, 0.0) + jnp.where(lane == i2, w2, 0.0)


def _router(x, router, tm=ROW_TILE):
    M, D = x.shape
    tm = math.gcd(M, tm)
    r = jnp.pad(router, ((0, 0), (0, LANES - router.shape[1])))
    return pl.pallas_call(
        _router_kernel,
        grid=(M // tm,),
        in_specs=[pl.BlockSpec((tm, D), lambda i: (i, 0)),
                  pl.BlockSpec((D, LANES), lambda i: (0, 0))],
        out_specs=pl.BlockSpec((tm, LANES), lambda i: (i, 0)),
        out_shape=jax.ShapeDtypeStruct((M, LANES), F32),
        compiler_params=_cparams("parallel"),
        name="moe_router",
    )(x, r)


def _ffn_kernel(*refs, n_e, n_f, routed):
    if routed:
        x_ref, comb_ref, wg_ref, wu_ref, wd_ref, g_ref, b_ref, o_ref, acc_ref, xb_ref = refs
    else:
        x_ref, wg_ref, wu_ref, wd_ref, g_ref, b_ref, o_ref, acc_ref, xb_ref = refs
    e = pl.program_id(1)
    f = pl.program_id(2)

    @pl.when((e == 0) & (f == 0))
    def _():
        acc_ref[...] = jnp.zeros_like(acc_ref)
        xb_ref[...] = x_ref[...].astype(BF16)

    xb = xb_ref[...]
    gate = _dot(xb, wg_ref[...])
    up = _dot(xb, wu_ref[...])
    act = gate * jax.nn.sigmoid(gate) * up
    if routed:
        comb = comb_ref[...]
        lane = lax.broadcasted_iota(jnp.int32, comb.shape, 1)
        act = act * jnp.sum(jnp.where(lane == e, comb, 0.0), -1, keepdims=True)
    acc_ref[...] += _dot(act.astype(BF16), wd_ref[...])

    @pl.when((e == n_e - 1) & (f == n_f - 1))
    def _():
        o_ref[...] = _layer_norm(ALPHA * x_ref[...] + acc_ref[...], g_ref[...], b_ref[...])


def _ffn_ln(x, w_gu, w_down, g, b, comb=None, tm=ROW_TILE, tf=1408):
    M, D = x.shape
    tm = math.gcd(M, tm)
    E, F = w_down.shape[0], w_down.shape[1]
    n_f = F // tf
    routed = comb is not None
    in_specs = [pl.BlockSpec((tm, D), lambda i, e, f: (i, 0))]
    args = [x]
    if routed:
        in_specs.append(pl.BlockSpec((tm, LANES), lambda i, e, f: (i, 0)))
        args.append(comb)
    in_specs += [pl.BlockSpec((None, D, tf), lambda i, e, f: (e, 0, f)),
                 pl.BlockSpec((None, D, tf), lambda i, e, f: (e, 0, f + n_f)),
                 pl.BlockSpec((None, tf, D), lambda i, e, f: (e, f, 0)),
                 pl.BlockSpec((1, D), lambda i, e, f: (0, 0)),
                 pl.BlockSpec((1, D), lambda i, e, f: (0, 0))]
    args += [w_gu, w_gu, w_down, g.reshape(1, D), b.reshape(1, D)]
    return pl.pallas_call(
        functools.partial(_ffn_kernel, n_e=E, n_f=n_f, routed=routed),
        grid=(M // tm, E, n_f),
        in_specs=in_specs,
        out_specs=pl.BlockSpec((tm, D), lambda i, e, f: (i, 0)),
        out_shape=jax.ShapeDtypeStruct((M, D), F32),
        scratch_shapes=[pltpu.VMEM((tm, D), F32), pltpu.VMEM((tm, D), BF16)],
        compiler_params=_cparams("parallel", "arbitrary", "arbitrary"),
        name="ffn_ln",
    )(*args)


def _xattn_kernel(q_ref, k_ref, v_ref, o_ref):
    for gi in range(q_ref.shape[0]):
        for h in range(MEM_HEADS):
            sl = slice(h * MEM_DH, (h + 1) * MEM_DH)
            qh = q_ref[gi, :, sl]
            kh = k_ref[gi, :, sl].astype(BF16)
            vh = v_ref[gi, :, sl].astype(BF16)
            s = _dot_nt(qh, kh)
            m = jnp.max(s, -1, keepdims=True)
            p = jnp.exp(s - m)
            p = p / jnp.sum(p, -1, keepdims=True)
            o_ref[gi, :, sl] = _dot(p.astype(BF16), vh).astype(o_ref.dtype)


def _xattn(q, k, v, rows, group):
    NB, R, HD = q.shape
    NM = k.shape[1]
    return pl.pallas_call(
        _xattn_kernel,
        grid=(NB // group, R // rows),
        in_specs=[pl.BlockSpec((group, rows, HD), lambda b, i: (b, i, 0)),
                  pl.BlockSpec((group, NM, HD), lambda b, i: (b, 0, 0)),
                  pl.BlockSpec((group, NM, HD), lambda b, i: (b, 0, 0))],
        out_specs=pl.BlockSpec((group, rows, HD), lambda b, i: (b, i, 0)),
        out_shape=jax.ShapeDtypeStruct((NB, R, HD), BF16),
        compiler_params=_cparams("parallel", "parallel"),
        name="xattn",
    )(q, k, v)


def _rope_tables(pos):
    half = MLA_ROPE // 2
    inv = ROPE_THETA ** (-jnp.arange(half, dtype=F32) / half)
    ang = pos.astype(F32)[:, None] * inv[None, :]
    c, s = jnp.cos(ang), jnp.sin(ang)
    return jnp.concatenate([c, c], -1), jnp.concatenate([-s, s], -1)


def _swap_halves(w):
    half = w.shape[-1] // 2
    return jnp.concatenate([w[..., half:], w[..., :half]], -1)


def _mla_down_kernel(x_ref, w_ref, qn_ref, kvn_ref, cos_ref, sin_ref,
                     cq_ref, ckv_ref, kpe_ref, ckvb_ref, kpeb_ref):
    down = _dot(x_ref[...].astype(BF16), w_ref[...])
    cq_ref[...] = _rms_norm(down[:, :MLA_Q_RANK], qn_ref[...]).astype(BF16)
    ckv = _rms_norm(down[:, MLA_Q_RANK:MLA_Q_RANK + MLA_KV_RANK], kvn_ref[...])
    ckv_ref[...] = ckv
    ckvb_ref[...] = ckv.astype(BF16)
    ra = down[:, 640:640 + MLA_ROPE]
    rb = down[:, 768:768 + MLA_ROPE]
    kpe = ra * cos_ref[...] + rb * sin_ref[...]
    kpe_ref[...] = kpe
    kpeb_ref[...] = kpe.astype(BF16)


def _mla_down(x, w_ext, q_norm, kv_norm, cos, sin, tm=ROW_TILE):
    M, D = x.shape
    tm = math.gcd(M, tm)
    NW = w_ext.shape[1]
    row = lambda n: pl.BlockSpec((tm, n), lambda i: (i, 0))
    full = lambda a, b: pl.BlockSpec((a, b), lambda i: (0, 0))
    return pl.pallas_call(
        _mla_down_kernel,
        grid=(M // tm,),
        in_specs=[row(D), full(D, NW), full(1, MLA_Q_RANK), full(1, MLA_KV_RANK),
                  row(MLA_ROPE), row(MLA_ROPE)],
        out_specs=[row(MLA_Q_RANK), row(MLA_KV_RANK), row(MLA_ROPE), row(MLA_KV_RANK), row(MLA_ROPE)],
        out_shape=[jax.ShapeDtypeStruct((M, MLA_Q_RANK), BF16),
                   jax.ShapeDtypeStruct((M, MLA_KV_RANK), F32),
                   jax.ShapeDtypeStruct((M, MLA_ROPE), F32),
                   jax.ShapeDtypeStruct((M, MLA_KV_RANK), BF16),
                   jax.ShapeDtypeStruct((M, MLA_ROPE), BF16)],
        compiler_params=_cparams("parallel"),
        name="mla_down",
    )(x, w_ext, q_norm.reshape(1, -1), kv_norm.reshape(1, -1), cos, sin)


def _mla_q_kernel(cq_ref, wq_ref, wk_ref, cos_ref, sin_ref, ql_ref, qp_ref):
    q = _dot(cq_ref[...], wq_ref[...])
    nope = q[:, :MLA_NOPE].astype(BF16)
    ra = q[:, MLA_NOPE:MLA_NOPE + MLA_ROPE]
    rb = q[:, MLA_NOPE + LANES:MLA_NOPE + LANES + MLA_ROPE]
    qp_ref[...] = ((ra * cos_ref[...] + rb * sin_ref[...]) * MLA_SCALE).astype(BF16)
    ql_ref[...] = (_dot_nt(nope, wk_ref[...]) * MLA_SCALE).astype(BF16)


def _mla_q(cq, wq_ext, wk, cos, sin, tm=ROW_TILE):
    M = cq.shape[0]
    tm = math.gcd(M, tm)
    H = wq_ext.shape[0]
    return pl.pallas_call(
        _mla_q_kernel,
        grid=(M // tm, H),
        in_specs=[pl.BlockSpec((tm, MLA_Q_RANK), lambda i, h: (i, 0)),
                  pl.BlockSpec((None, MLA_Q_RANK, MLA_NOPE + 2 * LANES), lambda i, h: (h, 0, 0)),
                  pl.BlockSpec((None, MLA_KV_RANK, MLA_NOPE), lambda i, h: (h, 0, 0)),
                  pl.BlockSpec((tm, MLA_ROPE), lambda i, h: (i, 0)),
                  pl.BlockSpec((tm, MLA_ROPE), lambda i, h: (i, 0))],
        out_specs=[pl.BlockSpec((None, tm, MLA_KV_RANK), lambda i, h: (h, i, 0)),
                   pl.BlockSpec((None, tm, MLA_ROPE), lambda i, h: (h, i, 0))],
        out_shape=[jax.ShapeDtypeStruct((H, M, MLA_KV_RANK), BF16),
                   jax.ShapeDtypeStruct((H, M, MLA_ROPE), BF16)],
        compiler_params=_cparams("parallel", "parallel"),
        name="mla_q",
    )(cq, wq_ext, wk, cos, sin)


def _online_softmax_step(s, v, m_ref, l_ref, acc_ref):
    m_prev = m_ref[...]
    m_new = jnp.maximum(m_prev, jnp.max(s, -1, keepdims=True))
    alpha = jnp.exp(m_prev - m_new)
    p = jnp.exp(s - m_new)
    l_ref[...] = alpha * l_ref[...] + jnp.sum(p, -1, keepdims=True)
    acc_ref[...] = alpha * acc_ref[...] + _dot(p.astype(BF16), v)
    m_ref[...] = m_new


def _mla_prompt_attn_kernel(ql_ref, qp_ref, ckv_ref, kpe_ref, o_ref, m_ref, l_ref, acc_ref, *, tk):
    H, TQ = ql_ref.shape[0], ql_ref.shape[1]
    q0 = pl.program_id(1) * TQ
    ql = ql_ref[...].reshape(H * TQ, MLA_KV_RANK)
    qp = qp_ref[...].reshape(H * TQ, MLA_ROPE)
    m_ref[...] = jnp.full_like(m_ref, NEG_INF)
    l_ref[...] = jnp.zeros_like(l_ref)
    acc_ref[...] = jnp.zeros_like(acc_ref)
    n_full = q0 // tk

    def chunk(kb, masked):
        k0 = pl.multiple_of(kb * tk, tk)
        kc = ckv_ref[pl.ds(k0, tk), :]
        kp = kpe_ref[pl.ds(k0, tk), :]
        s = _dot_nt(ql, kc) + _dot_nt(qp, kp)
        if masked:
            qpos = q0 + lax.broadcasted_iota(jnp.int32, (TQ, tk), 0)
            kpos = k0 + lax.broadcasted_iota(jnp.int32, (TQ, tk), 1)
            s = jnp.where((kpos <= qpos)[None], s.reshape(H, TQ, tk), NEG_INF).reshape(H * TQ, tk)
        _online_softmax_step(s, kc, m_ref, l_ref, acc_ref)

    def body(kb, carry):
        chunk(kb, False)
        return carry

    lax.fori_loop(0, n_full, body, 0)
    chunk(n_full, True)
    o_ref[...] = (acc_ref[...] / l_ref[...]).reshape(H, TQ, MLA_KV_RANK).astype(o_ref.dtype)


def _mla_prompt_attn(q_lat, q_pe, ckv_b, kpe_b, B, S, m_total, tk=512):
    H = q_lat.shape[0]
    tk = min(tk, S)
    nq = S // Q_BLOCK
    ckv3 = ckv_b[:B * S].reshape(B, S, MLA_KV_RANK)
    kpe3 = kpe_b[:B * S].reshape(B, S, MLA_ROPE)
    return pl.pallas_call(
        functools.partial(_mla_prompt_attn_kernel, tk=tk),
        grid=(B, nq),
        in_specs=[pl.BlockSpec((H, Q_BLOCK, MLA_KV_RANK), lambda b, i: (0, b * nq + i, 0)),
                  pl.BlockSpec((H, Q_BLOCK, MLA_ROPE), lambda b, i: (0, b * nq + i, 0)),
                  pl.BlockSpec((None, S, MLA_KV_RANK), lambda b, i: (b, 0, 0)),
                  pl.BlockSpec((None, S, MLA_ROPE), lambda b, i: (b, 0, 0))],
        out_specs=pl.BlockSpec((H, Q_BLOCK, MLA_KV_RANK), lambda b, i: (0, b * nq + i, 0)),
        out_shape=jax.ShapeDtypeStruct((H, m_total, MLA_KV_RANK), BF16),
        scratch_shapes=[pltpu.VMEM((H * Q_BLOCK, 1), F32), pltpu.VMEM((H * Q_BLOCK, 1), F32),
                        pltpu.VMEM((H * Q_BLOCK, MLA_KV_RANK), F32)],
        compiler_params=_cparams("parallel", "parallel"),
        name="mla_prompt_attn",
    )(q_lat, q_pe, ckv3, kpe3)


def _mla_sample_attn_kernel(pt_ref, ql_ref, qp_ref, cn_ref, kn_ref, *rest, pages, ds):
    ckv_refs = rest[:pages]
    kpe_refs = rest[pages:2 * pages]
    o_ref, m_ref, l_ref, acc_ref = rest[2 * pages:]
    j = pl.program_id(1)

    @pl.when(j == 0)
    def _():
        m_ref[...] = jnp.full_like(m_ref, NEG_INF)
        l_ref[...] = jnp.zeros_like(l_ref)
        acc_ref[...] = jnp.zeros_like(acc_ref)

    ql = ql_ref[...]
    qp = qp_ref[...]
    kcs = [r[...].astype(BF16) for r in ckv_refs]
    s = jnp.concatenate([_dot_nt(ql, kc) + _dot_nt(qp, kr[...].astype(BF16))
                         for kc, kr in zip(kcs, kpe_refs)], axis=1)
    m_prev = m_ref[...]
    m_new = jnp.maximum(m_prev, jnp.max(s, -1, keepdims=True))
    alpha = jnp.exp(m_prev - m_new)
    p = jnp.exp(s - m_new).astype(BF16)
    l_ref[...] = alpha * l_ref[...] + jnp.sum(p.astype(F32), -1, keepdims=True)
    pv = _dot(p[:, :PAGE_SIZE], kcs[0])
    for i in range(1, pages):
        pv += _dot(p[:, i * PAGE_SIZE:(i + 1) * PAGE_SIZE], kcs[i])
    acc_ref[...] = alpha * acc_ref[...] + pv
    m_ref[...] = m_new

    @pl.when(j == pl.num_programs(1) - 1)
    def _():
        cn = cn_ref[...].astype(BF16)
        s_new = _dot_nt(ql, cn) + _dot_nt(qp, kn_ref[...].astype(BF16))
        t = lax.broadcasted_iota(jnp.int32, s_new.shape, 0) % ds
        u = lax.broadcasted_iota(jnp.int32, s_new.shape, 1)
        s_new = jnp.where(u <= t, s_new, NEG_INF)
        _online_softmax_step(s_new, cn, m_ref, l_ref, acc_ref)
        o_ref[...] = (acc_ref[...] / l_ref[...]).astype(o_ref.dtype)


def _mla_sample_attn(q_lat_s, q_pe_s, cnew, knew, cache_ckv, cache_kpe, li, page_table, pages=8):
    DB, R, _ = q_lat_s.shape
    ds = cnew.shape[1]
    n_pages = page_table.shape[1]
    pages = min(pages, n_pages)

    def page_spec(width, i):
        return pl.BlockSpec((None, None, PAGE_SIZE, width),
                            lambda b, j, pt: (li, pt[b, j * pages + i], 0, 0))

    in_specs = [pl.BlockSpec((None, R, MLA_KV_RANK), lambda b, j, pt: (b, 0, 0)),
                pl.BlockSpec((None, R, MLA_ROPE), lambda b, j, pt: (b, 0, 0)),
                pl.BlockSpec((None, ds, MLA_KV_RANK), lambda b, j, pt: (b, 0, 0)),
                pl.BlockSpec((None, ds, MLA_ROPE), lambda b, j, pt: (b, 0, 0))]
    in_specs += [page_spec(MLA_KV_RANK, i) for i in range(pages)]
    in_specs += [page_spec(MLA_ROPE, i) for i in range(pages)]
    return pl.pallas_call(
        functools.partial(_mla_sample_attn_kernel, pages=pages, ds=ds),
        grid_spec=pltpu.PrefetchScalarGridSpec(
            num_scalar_prefetch=1,
            grid=(DB, n_pages // pages),
            in_specs=in_specs,
            out_specs=pl.BlockSpec((None, R, MLA_KV_RANK), lambda b, j, pt: (b, 0, 0)),
            scratch_shapes=[pltpu.VMEM((R, 1), F32), pltpu.VMEM((R, 1), F32),
                            pltpu.VMEM((R, MLA_KV_RANK), F32)]),
        out_shape=jax.ShapeDtypeStruct((DB, R, MLA_KV_RANK), BF16),
        compiler_params=_cparams("parallel", "arbitrary"),
        name="mla_sample_attn",
    )(page_table, q_lat_s, q_pe_s, cnew, knew, *([cache_ckv] * pages), *([cache_kpe] * pages))


def _mla_out_kernel(ol_ref, wuv_ref, wo_ref, res_ref, g_ref, b_ref, o_ref):
    acc = None
    for h in range(ol_ref.shape[0]):
        oh = _dot(ol_ref[h], wuv_ref[h]).astype(BF16)
        part = _dot(oh, wo_ref[h * MLA_VDIM:(h + 1) * MLA_VDIM, :])
        acc = part if acc is None else acc + part
    o_ref[...] = _layer_norm(ALPHA * res_ref[...] + acc, g_ref[...], b_ref[...])


def _mla_out(o_lat, w_uv, w_o, res, g, b, tm=ROW_TILE):
    H, M, R = o_lat.shape
    tm = math.gcd(M, tm)
    D = w_o.shape[1]
    return pl.pallas_call(
        _mla_out_kernel,
        grid=(M // tm,),
        in_specs=[pl.BlockSpec((H, tm, R), lambda i: (0, i, 0)),
                  pl.BlockSpec((H, R, MLA_VDIM), lambda i: (0, 0, 0)),
                  pl.BlockSpec((H * MLA_VDIM, D), lambda i: (0, 0)),
                  pl.BlockSpec((tm, D), lambda i: (i, 0)),
                  pl.BlockSpec((1, D), lambda i: (0, 0)),
                  pl.BlockSpec((1, D), lambda i: (0, 0))],
        out_specs=pl.BlockSpec((tm, D), lambda i: (i, 0)),
        out_shape=jax.ShapeDtypeStruct((M, D), F32),
        compiler_params=_cparams("parallel"),
        name="mla_out",
    )(o_lat, w_uv, w_o, res, g.reshape(1, D), b.reshape(1, D))


def _mla_layer(X, dims, cos, sin, cache_ckv, cache_kpe, li, page_table,
               w_down, q_norm, kv_norm, w_uq, w_uk, w_uv, w_o, g, b):
    B, S, DB, DS = dims
    MP = B * S
    M, D = X.shape
    H = MLA_HEADS
    rope_w = w_down[:, MLA_Q_RANK + MLA_KV_RANK:]
    zpad = jnp.zeros((D, LANES - MLA_ROPE), F32)
    w_ext = jnp.concatenate([w_down, zpad, _swap_halves(rope_w), zpad], 1).astype(BF16)
    cq, ckv, kpe, ckv_b, kpe_b = _mla_down(X, w_ext, q_norm, kv_norm, cos, sin)
    wq = w_uq.reshape(MLA_Q_RANK, H, MLA_NOPE + MLA_ROPE)
    hpad = jnp.zeros((MLA_Q_RANK, H, LANES - MLA_ROPE), F32)
    wq_ext = jnp.concatenate([wq, hpad, _swap_halves(wq[..., MLA_NOPE:]), hpad], -1)
    wq_ext = jnp.transpose(wq_ext, (1, 0, 2)).astype(BF16)
    wk = jnp.transpose(w_uk, (1, 0, 2)).astype(BF16)
    q_lat, q_pe = _mla_q(cq, wq_ext, wk, cos, sin)
    o_lat = _mla_prompt_attn(q_lat, q_pe, ckv_b, kpe_b, B, S, M)
    to_seq = lambda a: jnp.transpose(a[:, MP:].reshape(H, DB, DS, -1), (1, 0, 2, 3)).reshape(DB, H * DS, -1)
    o_s = _mla_sample_attn(to_seq(q_lat), to_seq(q_pe),
                           ckv[MP:].reshape(DB, DS, -1), kpe[MP:].reshape(DB, DS, -1),
                           cache_ckv, cache_kpe, li, page_table)
    o_s = jnp.transpose(o_s.reshape(DB, H, DS, -1), (1, 0, 2, 3)).reshape(H, DB * DS, -1)
    o_lat = lax.dynamic_update_slice(o_lat, o_s, (0, MP, 0))
    wuv = jnp.transpose(w_uv, (1, 0, 2)).astype(BF16)
    X = _mla_out(o_lat, wuv, w_o.astype(BF16), X, g, b)
    return X, ckv, kpe


def _t5_bucket(rel):
    n = jnp.maximum(rel, 0)
    exact = N_BUCKETS // 2
    large = exact + (jnp.log(jnp.maximum(n, 1).astype(F32) / exact)
                     / math.log(MAX_DISTANCE / exact) * (N_BUCKETS - exact)).astype(jnp.int32)
    return jnp.where(n < exact, n, jnp.minimum(large, N_BUCKETS - 1))


def _masked_softmax(s, mask):
    s = jnp.where(mask, s.astype(F32), -jnp.inf)
    m = jnp.max(s, -1, keepdims=True)
    m = jnp.where(jnp.isfinite(m), m, 0.0)
    e = jnp.exp(s - m)
    d = jnp.sum(e, -1, keepdims=True)
    return e / jnp.where(d > 0, d, 1.0)


def _compress_rows(rows, pe, w1, w2):
    B, T = rows.shape[:2]
    nh = T // CMP_STRIDE
    halves = rows[:, :nh * CMP_STRIDE].reshape(B, nh, CMP_STRIDE, NSA_GROUPS, NSA_DH)
    first = jnp.einsum('bnrgd,rde->bnge', halves, w1[:CMP_STRIDE])
    second = jnp.einsum('bnrgd,rde->bnge', halves, w1[CMP_STRIDE:])
    const = jnp.einsum('rd,rde->e', pe, w1)
    return jax.nn.silu(first[:, :-1] + second[:, 1:] + const) @ w2


def _selection_scores(p_grp, n_sel_blocks):
    nc = p_grp.shape[-1]
    ratio = SEL_BLOCK // CMP_STRIDE
    lead = CMP_LEN // CMP_STRIDE - 1
    span = ratio + lead
    total = ratio * n_sel_blocks + span
    P = jnp.pad(p_grp, [(0, 0)] * (p_grp.ndim - 1) + [(lead, total - lead - nc)])
    return sum(P[..., o:o + ratio * n_sel_blocks:ratio] for o in range(span))


def _nsa_attend(q, gates, q_pos, kc_blk, vc_blk, sel_gather, n_sel_blocks, kw, vw, kw_pos, table):
    B, S = q.shape[:2]
    nc = kc_blk.shape[1]
    table_g = table.reshape(N_BUCKETS, NSA_GROUPS, NSA_HPG)
    c_end = jnp.arange(nc, dtype=jnp.int32) * CMP_STRIDE + (CMP_LEN - 1)
    rel_c = q_pos[:, None] - c_end[None, :]
    bias_c = table_g[_t5_bucket(rel_c)].transpose(0, 2, 3, 1)
    s_c = jnp.einsum('bsgjd,bcgd->bsgjc', q, kc_blk).astype(F32) * NSA_SCALE + bias_c
    p_c = _masked_softmax(s_c, (rel_c >= 0)[:, None, None, :])
    o_c = jnp.einsum('bsgjc,bcgd->bsgjd', p_c.astype(vc_blk.dtype), vc_blk)
    imp = _selection_scores(jnp.sum(p_c, axis=3), n_sel_blocks)
    blk = jnp.arange(n_sel_blocks, dtype=jnp.int32)[None, :]
    cur = (q_pos // SEL_BLOCK)[:, None]
    forced = (blk == 0) | (blk == cur) | (blk == cur - 1)
    imp = jnp.where(forced[:, None, :], jnp.inf, jnp.where((blk <= cur)[:, None, :], imp, -jnp.inf))
    _, idx = lax.top_k(imp, min(N_SEL, n_sel_blocks))
    n = idx.shape[-1]
    ks, vs = sel_gather(idx)
    rel_s = q_pos[None, :, None, None, None] - (idx[..., None] * SEL_BLOCK + jnp.arange(SEL_BLOCK, dtype=jnp.int32))
    gi = jnp.arange(NSA_GROUPS)[None, None, :, None, None]
    bias_s = jnp.moveaxis(table_g[_t5_bucket(rel_s), gi], -1, 3).reshape(B, S, NSA_GROUPS, NSA_HPG, n * SEL_BLOCK)
    s_s = jnp.einsum('bsgjd,bsgnrd->bsgjnr', q, ks).reshape(B, S, NSA_GROUPS, NSA_HPG, n * SEL_BLOCK)
    s_s = s_s.astype(F32) * NSA_SCALE + bias_s
    p_s = _masked_softmax(s_s, (rel_s >= 0).reshape(B, S, NSA_GROUPS, 1, n * SEL_BLOCK))
    o_s = jnp.einsum('bsgjk,bsgkd->bsgjd', p_s.astype(vs.dtype), vs.reshape(B, S, NSA_GROUPS, n * SEL_BLOCK, NSA_DH))
    rel_w = q_pos[:, None] - kw_pos[None, :]
    bias_w = table_g[_t5_bucket(rel_w)].transpose(0, 2, 3, 1)
    s_w = jnp.einsum('bsgjd,bkgd->bsgjk', q, kw).astype(F32) * NSA_SCALE + bias_w
    mask_w = (rel_w >= 0) & (rel_w < WINDOW) & (kw_pos >= 0)[None, :]
    p_w = _masked_softmax(s_w, mask_w[:, None, None, :])
    o_w = jnp.einsum('bsgjk,bkgd->bsgjd', p_w.astype(vw.dtype), vw)
    o = (gates[:, :, 0, :, :, None] * o_c + gates[:, :, 1, :, :, None] * o_s
         + gates[:, :, 2, :, :, None] * o_w)
    return o.astype(q.dtype).reshape(B, S, NSA_HEADS * NSA_DH)


def _nsa_split(h, B, S):
    qw = NSA_HEADS * NSA_DH
    kvw = NSA_GROUPS * NSA_DH
    q = h[..., :qw].reshape(B, S, NSA_GROUPS, NSA_HPG, NSA_DH)
    kv = h[..., qw:qw + 6 * kvw].reshape(B, S, 6, NSA_GROUPS, NSA_DH)
    gates = jax.nn.sigmoid(h[..., qw + 6 * kvw:].astype(F32)).reshape(B, S, 3, NSA_GROUPS, NSA_HPG)
    return q, kv, gates


def _nsa_prompt_jnp(h, pe_k, w1_k, w2_k, pe_v, w1_v, w2_v, table):
    B, T = h.shape[:2]
    q, kv, gates = _nsa_split(h, B, T)
    kc, vc, ks, vs, kw, vw = (kv[:, :, j] for j in range(6))
    kc_blk = _compress_rows(kc, pe_k, w1_k, w2_k)
    vc_blk = _compress_rows(vc, pe_v, w1_v, w2_v)
    ns = T // SEL_BLOCK
    ks_b = ks.reshape(B, ns, SEL_BLOCK, NSA_GROUPS, NSA_DH)
    vs_b = vs.reshape(B, ns, SEL_BLOCK, NSA_GROUPS, NSA_DH)
    bi = jnp.arange(B)[:, None, None, None]
    gi = jnp.arange(NSA_GROUPS)[None, None, :, None]

    def sel_gather(idx):
        return ks_b[bi, idx, :, gi, :], vs_b[bi, idx, :, gi, :]

    wpad = ((0, 0), (WINDOW, 0), (0, 0), (0, 0))
    kw_pad = jnp.pad(kw, wpad)
    vw_pad = jnp.pad(vw, wpad)

    def block(q0):
        kar = jnp.arange(WINDOW + Q_BLOCK, dtype=jnp.int32)
        return _nsa_attend(lax.dynamic_slice_in_dim(q, q0, Q_BLOCK, 1),
                           lax.dynamic_slice_in_dim(gates, q0, Q_BLOCK, 1),
                           q0 + jnp.arange(Q_BLOCK, dtype=jnp.int32),
                           kc_blk, vc_blk, sel_gather, ns,
                           lax.dynamic_slice_in_dim(kw_pad, q0, WINDOW + Q_BLOCK, 1),
                           lax.dynamic_slice_in_dim(vw_pad, q0, WINDOW + Q_BLOCK, 1),
                           q0 - WINDOW + kar, table)

    out = lax.map(block, jnp.arange(T // Q_BLOCK, dtype=jnp.int32) * Q_BLOCK)
    out = jnp.moveaxis(out, 0, 1)
    o = out.reshape(out.shape[0], T, *out.shape[3:])
    keep = min(WINDOW, T)
    return o, kc, vc, ks, vs, kw[:, T - keep:], vw[:, T - keep:]


def _gather_pages(pool, li, page_table):
    g = pool[li, page_table]
    return g.reshape(g.shape[0], -1, *pool.shape[3:])


def _nsa_sample_jnp(h, li, cache_kc, cache_vc, cache_ks, cache_vs, state_kw, state_vw, page_table,
                    pe_k, w1_k, w2_k, pe_v, w1_v, w2_v, table):
    DB, S = h.shape[:2]
    past = page_table.shape[1] * PAGE_SIZE
    q_pos = past + jnp.arange(S, dtype=jnp.int32)
    q, kv, gates = _nsa_split(h, DB, S)
    kc, vc, ks, vs, kw, vw = (kv[:, :, j] for j in range(6))
    kc_blk = _compress_rows(jnp.concatenate([_gather_pages(cache_kc, li, page_table), kc], 1), pe_k, w1_k, w2_k)
    vc_blk = _compress_rows(jnp.concatenate([_gather_pages(cache_vc, li, page_table), vc], 1), pe_v, w1_v, w2_v)
    j0 = past // SEL_BLOCK
    nb_new = -(-S // SEL_BLOCK)
    spad = ((0, 0), (0, nb_new * SEL_BLOCK - S), (0, 0), (0, 0))
    ks_new = jnp.pad(ks, spad).reshape(DB, nb_new, SEL_BLOCK, NSA_GROUPS, NSA_DH)
    vs_new = jnp.pad(vs, spad).reshape(DB, nb_new, SEL_BLOCK, NSA_GROUPS, NSA_DH)
    sub = PAGE_SIZE // SEL_BLOCK
    n_pages = page_table.shape[1]
    bi = jnp.arange(DB)[:, None, None, None]
    gi = jnp.arange(NSA_GROUPS)[None, None, :, None]

    def sel_gather(idx):
        phys = page_table[bi, jnp.minimum(idx // sub, n_pages - 1)][..., None]
        rows = (idx % sub)[..., None] * SEL_BLOCK + jnp.arange(SEL_BLOCK, dtype=jnp.int32)
        g5 = gi[..., None]
        jn = jnp.clip(idx - j0, 0, nb_new - 1)
        is_new = (idx >= j0)[..., None, None]
        k = jnp.where(is_new, ks_new[bi, jn, :, gi, :], cache_ks[li, phys, rows, g5])
        v = jnp.where(is_new, vs_new[bi, jn, :, gi, :], cache_vs[li, phys, rows, g5])
        return k, v

    win_buf = state_kw.shape[2]
    kw_all = jnp.concatenate([state_kw[li], kw], 1)
    vw_all = jnp.concatenate([state_vw[li], vw], 1)
    kw_pos = past - win_buf + jnp.arange(win_buf + S, dtype=jnp.int32)
    o = _nsa_attend(q, gates, q_pos, kc_blk, vc_blk, sel_gather, j0 + nb_new, kw_all, vw_all, kw_pos, table)
    return o, kc, vc, ks, vs, kw_all[:, S:], vw_all[:, S:]


def _nsa_layer(X, dims, li, cache_kc, cache_vc, cache_ks, cache_vs, state_kw, state_vw, page_table,
               w_in, pe_k, w1_k, w2_k, pe_v, w1_v, w2_v, w_o, table, g, b):
    B, S, DB, DS = dims
    MP = B * S
    h = _mm(X, w_in.astype(BF16), name="nsa_in")
    cmp_w = (pe_k, w1_k, w2_k, pe_v, w1_v, w2_v)
    op, *outs_p = _nsa_prompt_jnp(h[:MP].reshape(B, S, -1), *cmp_w, table)
    os_, *outs_s = _nsa_sample_jnp(h[MP:].reshape(DB, DS, -1), li, cache_kc, cache_vc, cache_ks, cache_vs,
                                   state_kw, state_vw, page_table, *cmp_w, table)
    o = jnp.concatenate([op.reshape(MP, -1), os_.reshape(DB * DS, -1)], 0)
    X = _mm_ln(o, w_o.astype(BF16), X, g, b, name="nsa_out")
    return X, outs_p, outs_s


def kernel(x_prompt, x_sample, cache_mla_ckv, cache_mla_kpe, cache_nsa_kc, cache_nsa_vc, cache_nsa_ks, cache_nsa_vs, state_nsa_kw, state_nsa_vw, cache_mem_k, cache_mem_v, page_table, mem_prompt, ln_g, ln_b, mla_w_down, mla_q_norm, mla_kv_norm, mla_w_uq, mla_w_uk, mla_w_uv, mla_w_o, nsa_w_in, nsa_cmp_pe_k, nsa_cmp_w1_k, nsa_cmp_w2_k, nsa_cmp_pe_v, nsa_cmp_w1_v, nsa_cmp_w2_v, nsa_w_o, rel_bias, xa_w_q, xa_w_kv, xa_w_o, ffn_w_gu, ffn_w_down, moe_router, moe_w_gu, moe_w_down):
    B, S, D = x_prompt.shape
    DB, DS, _ = x_sample.shape
    dims = (B, S, DB, DS)
    MP, MS = B * S, DB * DS
    past = page_table.shape[1] * PAGE_SIZE
    n_mem = mem_prompt.shape[1]
    depth = ln_g.shape[0]
    HD = MEM_HEADS * MEM_DH

    X = jnp.concatenate([x_prompt.reshape(MP, D), x_sample.reshape(MS, D)], 0)
    pos = jnp.concatenate([jnp.tile(jnp.arange(S, dtype=jnp.int32), B),
                           jnp.tile(past + jnp.arange(DS, dtype=jnp.int32), DB)])
    cos, sin = _rope_tables(pos)
    mem_rows = mem_prompt.reshape(B * n_mem, D)

    outs_p = {k: [] for k in ("ckv", "kpe", "kc", "vc", "ks", "vs", "kw", "vw", "mk", "mv")}
    outs_s = {k: [] for k in ("ckv", "kpe", "kc", "vc", "ks", "vs", "kw", "vw")}
    for i in range(depth):
        li = i // 2
        if i % 2 == 0:
            X, ckv, kpe = _mla_layer(X, dims, cos, sin, cache_mla_ckv, cache_mla_kpe, li, page_table,
                                     mla_w_down[li], mla_q_norm[li], mla_kv_norm[li], mla_w_uq[li],
                                     mla_w_uk[li], mla_w_uv[li], mla_w_o[li], ln_g[i, 0], ln_b[i, 0])
            outs_p["ckv"].append(ckv[:MP].reshape(B, S, -1))
            outs_p["kpe"].append(kpe[:MP].reshape(B, S, -1))
            outs_s["ckv"].append(ckv[MP:].reshape(DB, DS, -1))
            outs_s["kpe"].append(kpe[MP:].reshape(DB, DS, -1))
        else:
            X, o_p, o_s = _nsa_layer(X, dims, li, cache_nsa_kc, cache_nsa_vc, cache_nsa_ks, cache_nsa_vs,
                                     state_nsa_kw, state_nsa_vw, page_table,
                                     nsa_w_in[li], nsa_cmp_pe_k[li], nsa_cmp_w1_k[li], nsa_cmp_w2_k[li],
                                     nsa_cmp_pe_v[li], nsa_cmp_w1_v[li], nsa_cmp_w2_v[li], nsa_w_o[li],
                                     rel_bias, ln_g[i, 0], ln_b[i, 0])
            for k, v in zip(("kc", "vc", "ks", "vs", "kw", "vw"), o_p):
                outs_p[k].append(v)
            for k, v in zip(("kc", "vc", "ks", "vs", "kw", "vw"), o_s):
                outs_s[k].append(v)
        kv = _mm(mem_rows, xa_w_kv[i].astype(BF16), name="mem_kv").reshape(B, n_mem, 2, MEM_HEADS, MEM_DH)
        mk, mv = kv[:, :, 0], kv[:, :, 1]
        outs_p["mk"].append(mk)
        outs_p["mv"].append(mv)
        q = _mm(X, xa_w_q[i].astype(BF16), out_dtype=BF16, scale=MEM_SCALE, name="xattn_q")
        o_p = _xattn(q[:MP].reshape(B, S, HD), mk.reshape(B, n_mem, HD), mv.reshape(B, n_mem, HD),
                     rows=min(ROW_TILE, S), group=1)
        o_s = _xattn(q[MP:].reshape(DB, DS, HD), cache_mem_k[i].reshape(DB, n_mem, HD),
                     cache_mem_v[i].reshape(DB, n_mem, HD), rows=DS, group=min(8, DB))
        o = jnp.concatenate([o_p.reshape(MP, HD), o_s.reshape(MS, HD)], 0)
        X = _mm_ln(o, xa_w_o[i].astype(BF16), X, ln_g[i, 1], ln_b[i, 1], name="xattn_out")
        if i % 2 == 0:
            X = _ffn_ln(X, ffn_w_gu[li][None].astype(BF16), ffn_w_down[li][None].astype(BF16),
                        ln_g[i, 2], ln_b[i, 2])
        else:
            comb = _router(X, moe_router[li])
            X = _ffn_ln(X, moe_w_gu[li].astype(BF16), moe_w_down[li].astype(BF16),
                        ln_g[i, 2], ln_b[i, 2], comb=comb)

    st = jnp.stack
    return (X[:MP].reshape(B, S, D), X[MP:].reshape(DB, DS, D),
            st(outs_p["ckv"]), st(outs_p["kpe"]), st(outs_p["kc"]), st(outs_p["vc"]),
            st(outs_p["ks"]), st(outs_p["vs"]), st(outs_p["kw"]), st(outs_p["vw"]),
            st(outs_p["mk"]), st(outs_p["mv"]),
            st(outs_s["ckv"]), st(outs_s["kpe"]), st(outs_s["kc"]), st(outs_s["vc"]),
            st(outs_s["ks"]), st(outs_s["vs"]), st(outs_s["kw"]), st(outs_s["vw"]))
```

```python
import functools
import math

import jax
import jax.numpy as jnp
import numpy as np
from jax import lax
from jax.experimental import pallas as pl
from jax.experimental.pallas import tpu as pltpu

F32 = jnp.float32
BF16 = jnp.bfloat16

DEPTH = 4
PAGE_SIZE = 128
MLA_HEADS = 16
MLA_Q_RANK = 384
MLA_KV_RANK = 256
MLA_NOPE = 128
MLA_ROPE = 64
MLA_VDIM = 128
MLA_SCALE = (MLA_NOPE + MLA_ROPE) ** -0.5
ROPE_THETA = 10000.0
NSA_HEADS = 16
NSA_GROUPS = 2
NSA_HPG = NSA_HEADS // NSA_GROUPS
NSA_DH = 64
NSA_SCALE = NSA_DH ** -0.5
CMP_LEN = 32
CMP_STRIDE = 16
SEL_BLOCK = 64
N_SEL = 16
WINDOW = 512
N_BUCKETS = 32
MAX_DISTANCE = 128
MEM_HEADS = 4
MEM_DH = 128
MEM_SCALE = MEM_DH ** -0.5
N_EXPERTS = 8
ALPHA = (2 * DEPTH) ** 0.25
LN_EPS = 1e-5
RMS_EPS = 1e-6
Q_BLOCK = 128

LANES = 128
VMEM_LIMIT = 56 * 1024 * 1024
ROW_TILE = 512
NEG_INF = float("-inf")
NEG_BIG = -1e30

SEL_CHUNK = 512
BAND = 2 * Q_BLOCK
CMP_BAND = 16
CMP_BAND_LEAD = 9
SEL_SHIFT = SEL_BLOCK.bit_length() - 1
HEAD_SHIFT = NSA_HEADS.bit_length() - 1
TILE_CMP, TILE_WIN, TILE_LAST, TILE_NEW = 0, 4 * LANES, 8 * LANES, 9 * LANES


def _cparams(*sem):
    return pltpu.CompilerParams(dimension_semantics=tuple(sem), vmem_limit_bytes=VMEM_LIMIT)


def _dot(a, b):
    return jnp.dot(a, b, preferred_element_type=F32)


def _dot_nt(a, b):
    return lax.dot_general(a, b, (((1,), (1,)), ((), ())), preferred_element_type=F32)


def _layer_norm(h, g, b):
    mu = jnp.mean(h, -1, keepdims=True)
    d = h - mu
    var = jnp.mean(d * d, -1, keepdims=True)
    return d * lax.rsqrt(var + LN_EPS) * g + b


def _rms_norm(x, g):
    return x * lax.rsqrt(jnp.mean(x * x, -1, keepdims=True) + RMS_EPS) * g


def _mm_kernel(x_ref, w_ref, o_ref, *, scale):
    acc = _dot(x_ref[...].astype(BF16), w_ref[...])
    if scale is not None:
        acc = acc * scale
    o_ref[...] = acc.astype(o_ref.dtype)


def _mm(x, w, out_dtype=F32, scale=None, tm=ROW_TILE, name="mm"):
    M, K = x.shape
    N = w.shape[1]
    tm = math.gcd(M, tm)
    return pl.pallas_call(
        functools.partial(_mm_kernel, scale=scale),
        grid=(M // tm,),
        in_specs=[pl.BlockSpec((tm, K), lambda i: (i, 0)),
                  pl.BlockSpec((K, N), lambda i: (0, 0))],
        out_specs=pl.BlockSpec((tm, N), lambda i: (i, 0)),
        out_shape=jax.ShapeDtypeStruct((M, N), out_dtype),
        compiler_params=_cparams("parallel"),
        name=name,
    )(x, w)


def _mm_ln_kernel(a_ref, w_ref, res_ref, g_ref, b_ref, o_ref):
    acc = _dot(a_ref[...].astype(BF16), w_ref[...])
    o_ref[...] = _layer_norm(ALPHA * res_ref[...] + acc, g_ref[...], b_ref[...])


def _mm_ln(a, w, res, g, b, tm=ROW_TILE, name="mm_ln"):
    M, K = a.shape
    D = w.shape[1]
    tm = math.gcd(M, tm)
    return pl.pallas_call(
        _mm_ln_kernel,
        grid=(M // tm,),
        in_specs=[pl.BlockSpec((tm, K), lambda i: (i, 0)),
                  pl.BlockSpec((K, D), lambda i: (0, 0)),
                  pl.BlockSpec((tm, D), lambda i: (i, 0)),
                  pl.BlockSpec((1, D), lambda i: (0, 0)),
                  pl.BlockSpec((1, D), lambda i: (0, 0))],
        out_specs=pl.BlockSpec((tm, D), lambda i: (i, 0)),
        out_shape=jax.ShapeDtypeStruct((M, D), F32),
        compiler_params=_cparams("parallel"),
        name=name,
    )(a, w, res, g.reshape(1, D), b.reshape(1, D))


def _router_kernel(x_ref, r_ref, o_ref):
    logits = jnp.dot(x_ref[...], r_ref[...], preferred_element_type=F32,
                     precision=lax.Precision.HIGHEST)
    lane = lax.broadcasted_iota(jnp.int32, logits.shape, 1)
    logits = jnp.where(lane < N_EXPERTS, logits, NEG_INF)
    v1 = jnp.max(logits, -1, keepdims=True)
    i1 = jnp.min(jnp.where(logits == v1, lane, LANES), -1, keepdims=True)
    rest = jnp.where(lane == i1, NEG_INF, logits)
    v2 = jnp.max(rest, -1, keepdims=True)
    i2 = jnp.min(jnp.where(rest == v2, lane, LANES), -1, keepdims=True)
    e2 = jnp.exp(v2 - v1)
    w1 = 1.0 / (1.0 + e2)
    w2 = e2 / (1.0 + e2)
    o_ref[...] = jnp.where(lane == i1, w1, 0.0) + jnp.where(lane == i2, w2, 0.0)


def _router(x, router, tm=ROW_TILE):
    M, D = x.shape
    tm = math.gcd(M, tm)
    r = jnp.pad(router, ((0, 0), (0, LANES - router.shape[1])))
    return pl.pallas_call(
        _router_kernel,
        grid=(M // tm,),
        in_specs=[pl.BlockSpec((tm, D), lambda i: (i, 0)),
                  pl.BlockSpec((D, LANES), lambda i: (0, 0))],
        out_specs=pl.BlockSpec((tm, LANES), lambda i: (i, 0)),
        out_shape=jax.ShapeDtypeStruct((M, LANES), F32),
        compiler_params=_cparams("parallel"),
        name="moe_router",
    )(x, r)


def _ffn_kernel(*refs, n_e, n_f, routed):
    if routed:
        x_ref, comb_ref, wg_ref, wu_ref, wd_ref, g_ref, b_ref, o_ref, acc_ref, xb_ref = refs
    else:
        x_ref, wg_ref, wu_ref, wd_ref, g_ref, b_ref, o_ref, acc_ref, xb_ref = refs
    e = pl.program_id(1)
    f = pl.program_id(2)

    @pl.when((e == 0) & (f == 0))
    def _():
        acc_ref[...] = jnp.zeros_like(acc_ref)
        xb_ref[...] = x_ref[...].astype(BF16)

    xb = xb_ref[...]
    gate = _dot(xb, wg_ref[...])
    up = _dot(xb, wu_ref[...])
    act = gate * jax.nn.sigmoid(gate) * up
    if routed:
        comb = comb_ref[...]
        lane = lax.broadcasted_iota(jnp.int32, comb.shape, 1)
        act = act * jnp.sum(jnp.where(lane == e, comb, 0.0), -1, keepdims=True)
    acc_ref[...] += _dot(act.astype(BF16), wd_ref[...])

    @pl.when((e == n_e - 1) & (f == n_f - 1))
    def _():
        o_ref[...] = _layer_norm(ALPHA * x_ref[...] + acc_ref[...], g_ref[...], b_ref[...])


def _ffn_ln(x, w_gu, w_down, g, b, comb=None, tm=ROW_TILE, tf=1408):
    M, D = x.shape
    tm = math.gcd(M, tm)
    E, F = w_down.shape[0], w_down.shape[1]
    n_f = F // tf
    routed = comb is not None
    in_specs = [pl.BlockSpec((tm, D), lambda i, e, f: (i, 0))]
    args = [x]
    if routed:
        in_specs.append(pl.BlockSpec((tm, LANES), lambda i, e, f: (i, 0)))
        args.append(comb)
    in_specs += [pl.BlockSpec((None, D, tf), lambda i, e, f: (e, 0, f)),
                 pl.BlockSpec((None, D, tf), lambda i, e, f: (e, 0, f + n_f)),
                 pl.BlockSpec((None, tf, D), lambda i, e, f: (e, f, 0)),
                 pl.BlockSpec((1, D), lambda i, e, f: (0, 0)),
                 pl.BlockSpec((1, D), lambda i, e, f: (0, 0))]
    args += [w_gu, w_gu, w_down, g.reshape(1, D), b.reshape(1, D)]
    return pl.pallas_call(
        functools.partial(_ffn_kernel, n_e=E, n_f=n_f, routed=routed),
        grid=(M // tm, E, n_f),
        in_specs=in_specs,
        out_specs=pl.BlockSpec((tm, D), lambda i, e, f: (i, 0)),
        out_shape=jax.ShapeDtypeStruct((M, D), F32),
        scratch_shapes=[pltpu.VMEM((tm, D), F32), pltpu.VMEM((tm, D), BF16)],
        compiler_params=_cparams("parallel", "arbitrary", "arbitrary"),
        name="ffn_ln",
    )(*args)


def _xattn_kernel(q_ref, k_ref, v_ref, o_ref):
    for gi in range(q_ref.shape[0]):
        for h in range(MEM_HEADS):
            sl = slice(h * MEM_DH, (h + 1) * MEM_DH)
            qh = q_ref[gi, :, sl]
            kh = k_ref[gi, :, sl].astype(BF16)
            vh = v_ref[gi, :, sl].astype(BF16)
            s = _dot_nt(qh, kh)
            m = jnp.max(s, -1, keepdims=True)
            p = jnp.exp(s - m)
            p = p / jnp.sum(p, -1, keepdims=True)
            o_ref[gi, :, sl] = _dot(p.astype(BF16), vh).astype(o_ref.dtype)


def _xattn(q, k, v, rows, group):
    NB, R, HD = q.shape
    NM = k.shape[1]
    return pl.pallas_call(
        _xattn_kernel,
        grid=(NB // group, R // rows),
        in_specs=[pl.BlockSpec((group, rows, HD), lambda b, i: (b, i, 0)),
                  pl.BlockSpec((group, NM, HD), lambda b, i: (b, 0, 0)),
                  pl.BlockSpec((group, NM, HD), lambda b, i: (b, 0, 0))],
        out_specs=pl.BlockSpec((group, rows, HD), lambda b, i: (b, i, 0)),
        out_shape=jax.ShapeDtypeStruct((NB, R, HD), BF16),
        compiler_params=_cparams("parallel", "parallel"),
        name="xattn",
    )(q, k, v)


def _rope_tables(pos):
    half = MLA_ROPE // 2
    inv = ROPE_THETA ** (-jnp.arange(half, dtype=F32) / half)
    ang = pos.astype(F32)[:, None] * inv[None, :]
    c, s = jnp.cos(ang), jnp.sin(ang)
    return jnp.concatenate([c, c], -1), jnp.concatenate([-s, s], -1)


def _swap_halves(w):
    half = w.shape[-1] // 2
    return jnp.concatenate([w[..., half:], w[..., :half]], -1)


def _mla_down_kernel(x_ref, w_ref, qn_ref, kvn_ref, cos_ref, sin_ref,
                     cq_ref, ckv_ref, kpe_ref, ckvb_ref, kpeb_ref):
    down = _dot(x_ref[...].astype(BF16), w_ref[...])
    cq_ref[...] = _rms_norm(down[:, :MLA_Q_RANK], qn_ref[...]).astype(BF16)
    ckv = _rms_norm(down[:, MLA_Q_RANK:MLA_Q_RANK + MLA_KV_RANK], kvn_ref[...])
    ckv_ref[...] = ckv
    ckvb_ref[...] = ckv.astype(BF16)
    ra = down[:, 640:640 + MLA_ROPE]
    rb = down[:, 768:768 + MLA_ROPE]
    kpe = ra * cos_ref[...] + rb * sin_ref[...]
    kpe_ref[...] = kpe
    kpeb_ref[...] = kpe.astype(BF16)


def _mla_down(x, w_ext, q_norm, kv_norm, cos, sin, tm=ROW_TILE):
    M, D = x.shape
    tm = math.gcd(M, tm)
    NW = w_ext.shape[1]
    row = lambda n: pl.BlockSpec((tm, n), lambda i: (i, 0))
    full = lambda a, b: pl.BlockSpec((a, b), lambda i: (0, 0))
    return pl.pallas_call(
        _mla_down_kernel,
        grid=(M // tm,),
        in_specs=[row(D), full(D, NW), full(1, MLA_Q_RANK), full(1, MLA_KV_RANK),
                  row(MLA_ROPE), row(MLA_ROPE)],
        out_specs=[row(MLA_Q_RANK), row(MLA_KV_RANK), row(MLA_ROPE), row(MLA_KV_RANK), row(MLA_ROPE)],
        out_shape=[jax.ShapeDtypeStruct((M, MLA_Q_RANK), BF16),
                   jax.ShapeDtypeStruct((M, MLA_KV_RANK), F32),
                   jax.ShapeDtypeStruct((M, MLA_ROPE), F32),
                   jax.ShapeDtypeStruct((M, MLA_KV_RANK), BF16),
                   jax.ShapeDtypeStruct((M, MLA_ROPE), BF16)],
        compiler_params=_cparams("parallel"),
        name="mla_down",
    )(x, w_ext, q_norm.reshape(1, -1), kv_norm.reshape(1, -1), cos, sin)


def _mla_q_kernel(cq_ref, wq_ref, wk_ref, cos_ref, sin_ref, ql_ref, qp_ref):
    q = _dot(cq_ref[...], wq_ref[...])
    nope = q[:, :MLA_NOPE].astype(BF16)
    ra = q[:, MLA_NOPE:MLA_NOPE + MLA_ROPE]
    rb = q[:, MLA_NOPE + LANES:MLA_NOPE + LANES + MLA_ROPE]
    qp_ref[...] = ((ra * cos_ref[...] + rb * sin_ref[...]) * MLA_SCALE).astype(BF16)
    ql_ref[...] = (_dot_nt(nope, wk_ref[...]) * MLA_SCALE).astype(BF16)


def _mla_q(cq, wq_ext, wk, cos, sin, tm=ROW_TILE):
    M = cq.shape[0]
    tm = math.gcd(M, tm)
    H = wq_ext.shape[0]
    return pl.pallas_call(
        _mla_q_kernel,
        grid=(M // tm, H),
        in_specs=[pl.BlockSpec((tm, MLA_Q_RANK), lambda i, h: (i, 0)),
                  pl.BlockSpec((None, MLA_Q_RANK, MLA_NOPE + 2 * LANES), lambda i, h: (h, 0, 0)),
                  pl.BlockSpec((None, MLA_KV_RANK, MLA_NOPE), lambda i, h: (h, 0, 0)),
                  pl.BlockSpec((tm, MLA_ROPE), lambda i, h: (i, 0)),
                  pl.BlockSpec((tm, MLA_ROPE), lambda i, h: (i, 0))],
        out_specs=[pl.BlockSpec((None, tm, MLA_KV_RANK), lambda i, h: (h, i, 0)),
                   pl.BlockSpec((None, tm, MLA_ROPE), lambda i, h: (h, i, 0))],
        out_shape=[jax.ShapeDtypeStruct((H, M, MLA_KV_RANK), BF16),
                   jax.ShapeDtypeStruct((H, M, MLA_ROPE), BF16)],
        compiler_params=_cparams("parallel", "parallel"),
        name="mla_q",
    )(cq, wq_ext, wk, cos, sin)


def _online_softmax_step(s, v, m_ref, l_ref, acc_ref):
    m_prev = m_ref[...]
    m_new = jnp.maximum(m_prev, jnp.max(s, -1, keepdims=True))
    alpha = jnp.exp(m_prev - m_new)
    p = jnp.exp(s - m_new)
    l_ref[...] = alpha * l_ref[...] + jnp.sum(p, -1, keepdims=True)
    acc_ref[...] = alpha * acc_ref[...] + _dot(p.astype(BF16), v)
    m_ref[...] = m_new


def _mla_prompt_attn_kernel(ql_ref, qp_ref, ckv_ref, kpe_ref, o_ref, m_ref, l_ref, acc_ref, *, tk):
    H, TQ = ql_ref.shape[0], ql_ref.shape[1]
    q0 = pl.program_id(1) * TQ
    ql = ql_ref[...].reshape(H * TQ, MLA_KV_RANK)
    qp = qp_ref[...].reshape(H * TQ, MLA_ROPE)
    m_ref[...] = jnp.full_like(m_ref, NEG_INF)
    l_ref[...] = jnp.zeros_like(l_ref)
    acc_ref[...] = jnp.zeros_like(acc_ref)
    n_full = q0 // tk

    def chunk(kb, masked):
        k0 = pl.multiple_of(kb * tk, tk)
        kc = ckv_ref[pl.ds(k0, tk), :]
        kp = kpe_ref[pl.ds(k0, tk), :]
        s = _dot_nt(ql, kc) + _dot_nt(qp, kp)
        if masked:
            qpos = q0 + lax.broadcasted_iota(jnp.int32, (TQ, tk), 0)
            kpos = k0 + lax.broadcasted_iota(jnp.int32, (TQ, tk), 1)
            s = jnp.where((kpos <= qpos)[None], s.reshape(H, TQ, tk), NEG_INF).reshape(H * TQ, tk)
        _online_softmax_step(s, kc, m_ref, l_ref, acc_ref)

    def body(kb, carry):
        chunk(kb, False)
        return carry

    lax.fori_loop(0, n_full, body, 0)
    chunk(n_full, True)
    o_ref[...] = (acc_ref[...] / l_ref[...]).reshape(H, TQ, MLA_KV_RANK).astype(o_ref.dtype)


def _mla_prompt_attn(q_lat, q_pe, ckv_b, kpe_b, B, S, m_total, tk=512):
    H = q_lat.shape[0]
    tk = min(tk, S)
    nq = S // Q_BLOCK
    ckv3 = ckv_b[:B * S].reshape(B, S, MLA_KV_RANK)
    kpe3 = kpe_b[:B * S].reshape(B, S, MLA_ROPE)
    return pl.pallas_call(
        functools.partial(_mla_prompt_attn_kernel, tk=tk),
        grid=(B, nq),
        in_specs=[pl.BlockSpec((H, Q_BLOCK, MLA_KV_RANK), lambda b, i: (0, b * nq + i, 0)),
                  pl.BlockSpec((H, Q_BLOCK, MLA_ROPE), lambda b, i: (0, b * nq + i, 0)),
                  pl.BlockSpec((None, S, MLA_KV_RANK), lambda b, i: (b, 0, 0)),
                  pl.BlockSpec((None, S, MLA_ROPE), lambda b, i: (b, 0, 0))],
        out_specs=pl.BlockSpec((H, Q_BLOCK, MLA_KV_RANK), lambda b, i: (0, b * nq + i, 0)),
        out_shape=jax.ShapeDtypeStruct((H, m_total, MLA_KV_RANK), BF16),
        scratch_shapes=[pltpu.VMEM((H * Q_BLOCK, 1), F32), pltpu.VMEM((H * Q_BLOCK, 1), F32),
                        pltpu.VMEM((H * Q_BLOCK, MLA_KV_RANK), F32)],
        compiler_params=_cparams("parallel", "parallel"),
        name="mla_prompt_attn",
    )(q_lat, q_pe, ckv3, kpe3)


def _mla_sample_attn_kernel(pt_ref, ql_ref, qp_ref, cn_ref, kn_ref, *rest, pages, ds):
    ckv_refs = rest[:pages]
    kpe_refs = rest[pages:2 * pages]
    o_ref, m_ref, l_ref, acc_ref = rest[2 * pages:]
    j = pl.program_id(1)

    @pl.when(j == 0)
    def _():
        m_ref[...] = jnp.full_like(m_ref, NEG_INF)
        l_ref[...] = jnp.zeros_like(l_ref)
        acc_ref[...] = jnp.zeros_like(acc_ref)

    ql = ql_ref[...]
    qp = qp_ref[...]
    kcs = [r[...].astype(BF16) for r in ckv_refs]
    s = jnp.concatenate([_dot_nt(ql, kc) + _dot_nt(qp, kr[...].astype(BF16))
                         for kc, kr in zip(kcs, kpe_refs)], axis=1)
    m_prev = m_ref[...]
    m_new = jnp.maximum(m_prev, jnp.max(s, -1, keepdims=True))
    alpha = jnp.exp(m_prev - m_new)
    p = jnp.exp(s - m_new).astype(BF16)
    l_ref[...] = alpha * l_ref[...] + jnp.sum(p.astype(F32), -1, keepdims=True)
    pv = _dot(p[:, :PAGE_SIZE], kcs[0])
    for i in range(1, pages):
        pv += _dot(p[:, i * PAGE_SIZE:(i + 1) * PAGE_SIZE], kcs[i])
    acc_ref[...] = alpha * acc_ref[...] + pv
    m_ref[...] = m_new

    @pl.when(j == pl.num_programs(1) - 1)
    def _():
        cn = cn_ref[...].astype(BF16)
        s_new = _dot_nt(ql, cn) + _dot_nt(qp, kn_ref[...].astype(BF16))
        t = lax.broadcasted_iota(jnp.int32, s_new.shape, 0) % ds
        u = lax.broadcasted_iota(jnp.int32, s_new.shape, 1)
        s_new = jnp.where(u <= t, s_new, NEG_INF)
        _online_softmax_step(s_new, cn, m_ref, l_ref, acc_ref)
        o_ref[...] = (acc_ref[...] / l_ref[...]).astype(o_ref.dtype)


def _mla_sample_attn(q_lat_s, q_pe_s, cnew, knew, cache_ckv, cache_kpe, li, page_table, pages=8):
    DB, R, _ = q_lat_s.shape
    ds = cnew.shape[1]
    n_pages = page_table.shape[1]
    pages = min(pages, n_pages)

    def page_spec(width, i):
        return pl.BlockSpec((None, None, PAGE_SIZE, width),
                            lambda b, j, pt: (li, pt[b, j * pages + i], 0, 0))

    in_specs = [pl.BlockSpec((None, R, MLA_KV_RANK), lambda b, j, pt: (b, 0, 0)),
                pl.BlockSpec((None, R, MLA_ROPE), lambda b, j, pt: (b, 0, 0)),
                pl.BlockSpec((None, ds, MLA_KV_RANK), lambda b, j, pt: (b, 0, 0)),
                pl.BlockSpec((None, ds, MLA_ROPE), lambda b, j, pt: (b, 0, 0))]
    in_specs += [page_spec(MLA_KV_RANK, i) for i in range(pages)]
    in_specs += [page_spec(MLA_ROPE, i) for i in range(pages)]
    return pl.pallas_call(
        functools.partial(_mla_sample_attn_kernel, pages=pages, ds=ds),
        grid_spec=pltpu.PrefetchScalarGridSpec(
            num_scalar_prefetch=1,
            grid=(DB, n_pages // pages),
            in_specs=in_specs,
            out_specs=pl.BlockSpec((None, R, MLA_KV_RANK), lambda b, j, pt: (b, 0, 0)),
            scratch_shapes=[pltpu.VMEM((R, 1), F32), pltpu.VMEM((R, 1), F32),
                            pltpu.VMEM((R, MLA_KV_RANK), F32)]),
        out_shape=jax.ShapeDtypeStruct((DB, R, MLA_KV_RANK), BF16),
        compiler_params=_cparams("parallel", "arbitrary"),
        name="mla_sample_attn",
    )(page_table, q_lat_s, q_pe_s, cnew, knew, *([cache_ckv] * pages), *([cache_kpe] * pages))


def _mla_out_kernel(ol_ref, wuv_ref, wo_ref, res_ref, g_ref, b_ref, o_ref):
    acc = None
    for h in range(ol_ref.shape[0]):
        oh = _dot(ol_ref[h], wuv_ref[h]).astype(BF16)
        part = _dot(oh, wo_ref[h * MLA_VDIM:(h + 1) * MLA_VDIM, :])
        acc = part if acc is None else acc + part
    o_ref[...] = _layer_norm(ALPHA * res_ref[...] + acc, g_ref[...], b_ref[...])


def _mla_out(o_lat, w_uv, w_o, res, g, b, tm=ROW_TILE):
    H, M, R = o_lat.shape
    tm = math.gcd(M, tm)
    D = w_o.shape[1]
    return pl.pallas_call(
        _mla_out_kernel,
        grid=(M // tm,),
        in_specs=[pl.BlockSpec((H, tm, R), lambda i: (0, i, 0)),
                  pl.BlockSpec((H, R, MLA_VDIM), lambda i: (0, 0, 0)),
                  pl.BlockSpec((H * MLA_VDIM, D), lambda i: (0, 0)),
                  pl.BlockSpec((tm, D), lambda i: (i, 0)),
                  pl.BlockSpec((1, D), lambda i: (0, 0)),
                  pl.BlockSpec((1, D), lambda i: (0, 0))],
        out_specs=pl.BlockSpec((tm, D), lambda i: (i, 0)),
        out_shape=jax.ShapeDtypeStruct((M, D), F32),
        compiler_params=_cparams("parallel"),
        name="mla_out",
    )(o_lat, w_uv, w_o, res, g.reshape(1, D), b.reshape(1, D))


def _mla_layer(X, dims, cos, sin, cache_ckv, cache_kpe, li, page_table,
               w_down, q_norm, kv_norm, w_uq, w_uk, w_uv, w_o, g, b):
    B, S, DB, DS = dims
    MP = B * S
    M, D = X.shape
    H = MLA_HEADS
    rope_w = w_down[:, MLA_Q_RANK + MLA_KV_RANK:]
    zpad = jnp.zeros((D, LANES - MLA_ROPE), F32)
    w_ext = jnp.concatenate([w_down, zpad, _swap_halves(rope_w), zpad], 1).astype(BF16)
    cq, ckv, kpe, ckv_b, kpe_b = _mla_down(X, w_ext, q_norm, kv_norm, cos, sin)
    wq = w_uq.reshape(MLA_Q_RANK, H, MLA_NOPE + MLA_ROPE)
    hpad = jnp.zeros((MLA_Q_RANK, H, LANES - MLA_ROPE), F32)
    wq_ext = jnp.concatenate([wq, hpad, _swap_halves(wq[..., MLA_NOPE:]), hpad], -1)
    wq_ext = jnp.transpose(wq_ext, (1, 0, 2)).astype(BF16)
    wk = jnp.transpose(w_uk, (1, 0, 2)).astype(BF16)
    q_lat, q_pe = _mla_q(cq, wq_ext, wk, cos, sin)
    o_lat = _mla_prompt_attn(q_lat, q_pe, ckv_b, kpe_b, B, S, M)
    to_seq = lambda a: jnp.transpose(a[:, MP:].reshape(H, DB, DS, -1), (1, 0, 2, 3)).reshape(DB, H * DS, -1)
    o_s = _mla_sample_attn(to_seq(q_lat), to_seq(q_pe),
                           ckv[MP:].reshape(DB, DS, -1), kpe[MP:].reshape(DB, DS, -1),
                           cache_ckv, cache_kpe, li, page_table)
    o_s = jnp.transpose(o_s.reshape(DB, H, DS, -1), (1, 0, 2, 3)).reshape(H, DB * DS, -1)
    o_lat = lax.dynamic_update_slice(o_lat, o_s, (0, MP, 0))
    wuv = jnp.transpose(w_uv, (1, 0, 2)).astype(BF16)
    X = _mla_out(o_lat, wuv, w_o.astype(BF16), X, g, b)
    return X, ckv, kpe


def _np_t5_bucket(rel):
    n = np.maximum(rel, 0)
    exact = N_BUCKETS // 2
    large = exact + (np.log(np.maximum(n, 1) / exact) / math.log(MAX_DISTANCE / exact)
                     * (N_BUCKETS - exact)).astype(np.int64)
    return np.where(n < exact, n, np.minimum(large, N_BUCKETS - 1)).astype(np.int32)


def _bias_tile_kernel(tab_ref, bkt_ref, o_ref):
    bkt = bkt_ref[...]
    for h in range(NSA_HEADS):
        far = tab_ref[N_BUCKETS - 1, h]
        acc = jnp.zeros(bkt.shape, F32)
        for b in range(N_BUCKETS - 1):
            acc = jnp.where(bkt == b, tab_ref[b, h] - far, acc)
        o_ref[h] = acc


def _bias_tiles(table, buckets):
    R, C = buckets.shape
    return pl.pallas_call(
        _bias_tile_kernel,
        in_specs=[pl.BlockSpec(memory_space=pltpu.SMEM), pl.BlockSpec(memory_space=pltpu.VMEM)],
        out_specs=pl.BlockSpec(memory_space=pltpu.VMEM),
        out_shape=jax.ShapeDtypeStruct((NSA_HEADS, R, C), F32),
        name="nsa_bias_tiles",
    )(table, jnp.asarray(buckets))


def _hi_lo(a):
    hi = a.astype(BF16)
    lo = (a - hi.astype(F32)).astype(BF16)
    return hi, lo


def _nsa_in_kernel(x_ref, w_ref, q_ref, kv_ref, kvb_ref, gate_ref):
    h = _dot(x_ref[...].astype(BF16), w_ref[...])
    qw = NSA_HEADS * LANES
    for hd in range(NSA_HEADS):
        q_ref[hd] = (h[:, hd * LANES:(hd + 1) * LANES] * NSA_SCALE).astype(BF16)
    for j in range(6):
        piece = h[:, qw + j * LANES:qw + (j + 1) * LANES]
        kv_ref[j] = piece
        kvb_ref[j] = piece.astype(BF16)
    gate_ref[...] = jax.nn.sigmoid(h[:, qw + 6 * LANES:])


def _nsa_in(x, w_ext, tm=ROW_TILE):
    M, D = x.shape
    tm = math.gcd(M, tm)
    NW = w_ext.shape[1]
    return pl.pallas_call(
        _nsa_in_kernel,
        grid=(M // tm,),
        in_specs=[pl.BlockSpec((tm, D), lambda i: (i, 0)),
                  pl.BlockSpec((D, NW), lambda i: (0, 0))],
        out_specs=[pl.BlockSpec((NSA_HEADS, tm, LANES), lambda i: (0, i, 0)),
                   pl.BlockSpec((6, tm, LANES), lambda i: (0, i, 0)),
                   pl.BlockSpec((6, tm, LANES), lambda i: (0, i, 0)),
                   pl.BlockSpec((tm, LANES), lambda i: (i, 0))],
        out_shape=[jax.ShapeDtypeStruct((NSA_HEADS, M, LANES), BF16),
                   jax.ShapeDtypeStruct((6, M, LANES), F32),
                   jax.ShapeDtypeStruct((6, M, LANES), BF16),
                   jax.ShapeDtypeStruct((M, LANES), F32)],
        compiler_params=_cparams("parallel"),
        name="nsa_in",
    )(x, w_ext)


def _pad_heads_to_group_lanes(w, axis):
    z = jnp.zeros_like(w)
    lo = jnp.concatenate([w, z], axis + 1)
    hi = jnp.concatenate([z, w], axis + 1)
    shape = [1] * w.ndim
    shape[axis] = NSA_HEADS
    first = (jnp.arange(NSA_HEADS) < NSA_HPG).reshape(shape)
    return jnp.where(first, lo, hi)


def _compress_weights(pe, w1, w2):
    eye = jnp.eye(NSA_GROUPS, dtype=F32)
    half = lambda w: jnp.einsum('rde,gh->rgdhe', w, eye).reshape(CMP_STRIDE * LANES, LANES)
    w_big = jnp.concatenate([half(w1[:CMP_STRIDE]), half(w1[CMP_STRIDE:])], 1)
    flat = lambda p: jnp.tile(p[:, None, :], (1, NSA_GROUPS, 1)).reshape(1, CMP_STRIDE * LANES)
    pe2 = jnp.concatenate([flat(pe[:CMP_STRIDE]), flat(pe[CMP_STRIDE:]),
                           jnp.zeros((6, CMP_STRIDE * LANES), F32)], 0)
    return w_big.astype(BF16), pe2.astype(BF16), jnp.kron(eye, w2).astype(BF16)


def _compress_blocks(fs, const, w2):
    nh = fs.shape[0]
    nxt = pltpu.roll(fs[:, LANES:], nh - 1, 0)
    pre = fs[:, :LANES] + nxt + const
    out = _dot((pre * jax.nn.sigmoid(pre)).astype(BF16), w2)
    row = lax.broadcasted_iota(jnp.int32, out.shape, 0)
    return jnp.where(row < nh - 1, out, 0.0).astype(BF16)


def _nsa_compress_kernel(x_ref, w_ref, pe_ref, w2_ref, o_ref, c_ref):
    fs = _dot(x_ref[...], w_ref[...])
    cst = _dot(pe_ref[...], w_ref[...])
    const = cst[0:1, :LANES] + cst[1:2, LANES:]
    c_ref[...] = jnp.broadcast_to(const, c_ref.shape)
    o_ref[...] = _compress_blocks(fs, const, w2_ref[...])


def _nsa_compress(x, w_big, pe2, w2):
    _, B, nh, K = x.shape
    return pl.pallas_call(
        _nsa_compress_kernel,
        grid=(2, B),
        in_specs=[pl.BlockSpec((None, None, nh, K), lambda t, b: (t, b, 0, 0)),
                  pl.BlockSpec((None, K, 2 * LANES), lambda t, b: (t, 0, 0)),
                  pl.BlockSpec((None, 8, K), lambda t, b: (t, 0, 0)),
                  pl.BlockSpec((None, LANES, LANES), lambda t, b: (t, 0, 0))],
        out_specs=[pl.BlockSpec((None, None, nh, LANES), lambda t, b: (t, b, 0, 0)),
                   pl.BlockSpec((None, None, 8, LANES), lambda t, b: (t, b, 0, 0))],
        out_shape=[jax.ShapeDtypeStruct((2, B, nh, LANES), BF16),
                   jax.ShapeDtypeStruct((2, B, 8, LANES), F32)],
        compiler_params=_cparams("parallel", "parallel"),
        name="nsa_compress",
    )(x, w_big, pe2, w2)


def _top_blocks(work, blk, n_pick):
    ns = work.shape[-1]
    sel = jnp.zeros(work.shape, F32)
    for _ in range(n_pick):
        best = jnp.max(work, -1, keepdims=True)
        first = jnp.min(jnp.where(work == best, blk, ns), -1, keepdims=True)
        hit = blk == first
        sel = jnp.where(hit, 1.0, sel)
        work = jnp.where(hit, NEG_INF, work)
    return sel


def _block_importance(p_grp, ns):
    nh = p_grp.shape[-1]
    cc = lax.broadcasted_iota(jnp.int32, (nh, ns), 0)
    jj = lax.broadcasted_iota(jnp.int32, (nh, ns), 1) * (SEL_BLOCK // CMP_STRIDE)
    cover = jnp.where((cc >= jj - 1) & (cc <= jj + 3), 1.0, 0.0)
    return jnp.dot(p_grp, cover, preferred_element_type=F32, precision=lax.Precision.HIGHEST)


def _nsa_prompt_kernel(q_ref, kc_ref, vc_ref, ks_ref, vs_ref, kw_ref, vw_ref, gate_ref, akey_ref, aband_ref,
                       o_ref, m_ref, l_ref, acc_ref, *, tk):
    H, TQ = q_ref.shape[0], q_ref.shape[1]
    R = H * TQ
    G = NSA_GROUPS
    S = ks_ref.shape[0]
    nh = kc_ref.shape[0]
    ns = S // SEL_BLOCK
    qb = pl.program_id(1)
    q0 = qb * TQ
    q = q_ref[...].reshape(R, LANES)
    akey = akey_ref[...].reshape(R, 2 * BAND)
    i32 = jnp.int32

    def key_band(k_first, n_keys):
        d = k_first - q0 + Q_BLOCK + lax.broadcasted_iota(i32, (n_keys, 2 * BAND), 0)
        slot = lax.broadcasted_iota(i32, (n_keys, 2 * BAND), 1) & (BAND - 1)
        return _dot_nt(akey, jnp.where(d == slot, 1.0, 0.0).astype(BF16))

    c_oh = lax.broadcasted_iota(i32, (nh, LANES), 0) - (qb * (Q_BLOCK // CMP_STRIDE) - CMP_BAND_LEAD)
    m_oh = lax.broadcasted_iota(i32, (nh, LANES), 1)
    oh = jnp.where((c_oh == (m_oh & (CMP_BAND - 1))) & (m_oh < 2 * CMP_BAND), 1.0, 0.0).astype(BF16)
    s_c = _dot_nt(q, kc_ref[...]) + _dot_nt(aband_ref[...].reshape(R, LANES), oh)
    c_end = lax.broadcasted_iota(i32, (TQ, nh), 1) * CMP_STRIDE + (CMP_LEN - 1)
    q_pos = q0 + lax.broadcasted_iota(i32, (TQ, nh), 0)
    s_c = s_c.reshape(H, TQ, nh) + jnp.where(c_end <= q_pos, 0.0, NEG_INF)[None]
    m_c = jnp.max(s_c, -1, keepdims=True)
    m_c = jnp.where(m_c == NEG_INF, 0.0, m_c)
    e_c = jnp.exp(s_c - m_c)
    d_c = jnp.sum(e_c, -1, keepdims=True)
    p_c = e_c / jnp.where(d_c > 0, d_c, 1.0)
    o_c = _dot(p_c.reshape(R, nh).astype(BF16), vc_ref[...])

    p_grp = jnp.sum(p_c.reshape(G, NSA_HPG, TQ, nh), axis=1).reshape(G * TQ, nh)
    imp = _block_importance(p_grp, ns)
    blk = lax.broadcasted_iota(i32, (G * TQ, ns), 1)
    cur = (q0 + (lax.broadcasted_iota(i32, (G * TQ, ns), 0) & (TQ - 1))) >> SEL_SHIFT
    forced = (blk == 0) | (blk == cur) | (blk == cur - 1)
    work = jnp.where(forced, jnp.inf, jnp.where(blk <= cur, imp, NEG_INF))
    sel_b = _top_blocks(work, blk, min(N_SEL, ns)).astype(BF16)

    m_ref[...] = jnp.full_like(m_ref, NEG_BIG)
    l_ref[...] = jnp.zeros_like(l_ref)
    acc_ref[...] = jnp.zeros_like(acc_ref)

    def sel_chunk(kb, near):
        k0 = pl.multiple_of(kb * tk, tk)
        s = _dot_nt(q, ks_ref[pl.ds(k0, tk), :])
        eb = lax.broadcasted_iota(i32, (ns, tk), 0)
        ek = (k0 + lax.broadcasted_iota(i32, (ns, tk), 1)) >> SEL_SHIFT
        allow = _dot(sel_b, jnp.where(eb == ek, 1.0, 0.0).astype(BF16))
        mask = (allow - 1.0) * (-NEG_BIG)
        if near:
            s = s + key_band(k0, tk)
            qpos = q0 + (lax.broadcasted_iota(i32, (G * TQ, tk), 0) & (TQ - 1))
            kpos = k0 + lax.broadcasted_iota(i32, (G * TQ, tk), 1)
            mask = jnp.where(kpos <= qpos, mask, NEG_BIG)
        s = (s.reshape(G, NSA_HPG, TQ, tk) + mask.reshape(G, 1, TQ, tk)).reshape(R, tk)
        _online_softmax_step(s, vs_ref[pl.ds(k0, tk), :], m_ref, l_ref, acc_ref)

    n_last = q0 // tk
    n_far = jnp.maximum(q0 - Q_BLOCK, 0) // tk

    def far_body(kb, carry):
        sel_chunk(kb, False)
        return carry

    lax.fori_loop(0, n_far, far_body, 0)

    @pl.when(n_far < n_last)
    def _():
        sel_chunk(n_far, True)

    sel_chunk(n_last, True)
    o_s = acc_ref[...] / l_ref[...]

    nw = WINDOW + TQ
    w0 = pl.multiple_of(jnp.maximum(q0 - WINDOW, 0), TQ)
    s_w = _dot_nt(q, kw_ref[pl.ds(w0, nw), :]) + key_band(w0, nw)
    rel = q0 + lax.broadcasted_iota(i32, (TQ, nw), 0) - (w0 + lax.broadcasted_iota(i32, (TQ, nw), 1))
    s_w = s_w.reshape(H, TQ, nw) + jnp.where((rel >= 0) & (rel < WINDOW), 0.0, NEG_INF)[None]
    e_w = jnp.exp(s_w - jnp.max(s_w, -1, keepdims=True))
    d_w = jnp.sum(e_w, -1, keepdims=True).reshape(R, 1)
    o_w = _dot(e_w.reshape(R, nw).astype(BF16), vw_ref[pl.ds(w0, nw), :]) / d_w

    gates = gate_ref[...]
    for h in range(H):
        rows = slice(h * TQ, (h + 1) * TQ)
        o_h = (gates[:, h:h + 1] * o_c[rows] + gates[:, H + h:H + h + 1] * o_s[rows]
               + gates[:, 2 * H + h:2 * H + h + 1] * o_w[rows])
        o_ref[:, h * LANES:(h + 1) * LANES] = o_h.astype(o_ref.dtype)


def _nsa_prompt_attn(q, cmp_kv, kvb, gates, akey, aband, B, S, m_total):
    H = q.shape[0]
    nq = S // Q_BLOCK
    nh = cmp_kv.shape[2]
    tk = min(SEL_CHUNK, S)
    kv3 = kvb[:, :B * S].reshape(6, B, S, LANES)

    def stream(j):
        return pl.BlockSpec((None, None, S, LANES), lambda b, i: (j, b, 0, 0))

    def cmp(j):
        return pl.BlockSpec((None, None, nh, LANES), lambda b, i: (j, b, 0, 0))

    return pl.pallas_call(
        functools.partial(_nsa_prompt_kernel, tk=tk),
        grid=(B, nq),
        in_specs=[pl.BlockSpec((H, Q_BLOCK, LANES), lambda b, i: (0, b * nq + i, 0)),
                  cmp(0), cmp(1), stream(2), stream(3), stream(4), stream(5),
                  pl.BlockSpec((Q_BLOCK, LANES), lambda b, i: (b * nq + i, 0)),
                  pl.BlockSpec((H, Q_BLOCK, 2 * BAND), lambda b, i: (0, 0, 0)),
                  pl.BlockSpec((H, Q_BLOCK, LANES), lambda b, i: (0, 0, 0))],
        out_specs=pl.BlockSpec((Q_BLOCK, H * LANES), lambda b, i: (b * nq + i, 0)),
        out_shape=jax.ShapeDtypeStruct((m_total, H * LANES), BF16),
        scratch_shapes=[pltpu.VMEM((H * Q_BLOCK, 1), F32), pltpu.VMEM((H * Q_BLOCK, 1), F32),
                        pltpu.VMEM((H * Q_BLOCK, LANES), F32)],
        compiler_params=_cparams("parallel", "parallel"),
        name="nsa_prompt_attn",
    )(q, cmp_kv, cmp_kv, kv3, kv3, kv3, kv3, gates, akey, aband)


def _nsa_fs_kernel(pt_ref, w_ref, *rest):
    page_refs, o_ref = rest[:-1], rest[-1]
    x = jnp.concatenate([r[...] for r in page_refs], axis=0).astype(BF16)
    o_ref[...] = _dot(x, w_ref[...])


def _nsa_fs_sample(cache, li, page_table, w_big, pages=16):
    L, POOL = cache.shape[:2]
    DB, n_pages = page_table.shape
    pages = min(pages, n_pages)
    hb = PAGE_SIZE // CMP_STRIDE
    K = CMP_STRIDE * LANES
    view = cache.reshape(L, POOL, hb, K)

    def page_spec(i):
        return pl.BlockSpec((None, None, hb, K), lambda b, j, pt: (li, pt[b, j * pages + i], 0, 0))

    return pl.pallas_call(
        _nsa_fs_kernel,
        grid_spec=pltpu.PrefetchScalarGridSpec(
            num_scalar_prefetch=1,
            grid=(DB, n_pages // pages),
            in_specs=[pl.BlockSpec((K, 2 * LANES), lambda b, j, pt: (0, 0))] + [page_spec(i) for i in range(pages)],
            out_specs=pl.BlockSpec((None, pages * hb, 2 * LANES), lambda b, j, pt: (b, j, 0))),
        out_shape=jax.ShapeDtypeStruct((DB, n_pages * hb, 2 * LANES), F32),
        compiler_params=_cparams("parallel", "parallel"),
        name="nsa_fs_sample",
    )(page_table, w_big, *([view] * pages))


def _nsa_sample_kernel(pt_ref, q_ref, fsk_ref, fsv_ref, const_ref, w2_ref, skw_ref, svw_ref, new_ref, gate_ref,
                       tile_ref, *rest, pages, ds, past):
    ks_refs = rest[:pages]
    vs_refs = rest[pages:2 * pages]
    o_ref, sel_ref, oc_ref, m_ref, l_ref, acc_ref = rest[2 * pages:]
    j = pl.program_id(1)
    n_steps = pl.num_programs(1)
    H, G = NSA_HEADS, NSA_GROUPS
    R = ds * H
    nh = fsk_ref.shape[0]
    ns = past // SEL_BLOCK
    win = skw_ref.shape[0]
    i32 = jnp.int32
    q = q_ref[...]
    tok = lax.broadcasted_iota(i32, (R, 1), 0) >> HEAD_SHIFT

    @pl.when(j == 0)
    def _():
        kcb = _compress_blocks(fsk_ref[...], const_ref[0], w2_ref[0])
        vcb = _compress_blocks(fsv_ref[...], const_ref[1], w2_ref[1])
        s_c = _dot_nt(q, kcb) + tile_ref[:, TILE_CMP:TILE_CMP + nh]
        c_end = lax.broadcasted_iota(i32, (R, nh), 1) * CMP_STRIDE + (CMP_LEN - 1)
        s_c = jnp.where(c_end <= past + tok, s_c, NEG_INF)
        m_c = jnp.max(s_c, -1, keepdims=True)
        m_c = jnp.where(m_c == NEG_INF, 0.0, m_c)
        e_c = jnp.exp(s_c - m_c)
        d_c = jnp.sum(e_c, -1, keepdims=True)
        p_c = e_c / jnp.where(d_c > 0, d_c, 1.0)
        oc_ref[...] = _dot(p_c.astype(BF16), vcb)
        p_grp = jnp.sum(p_c.reshape(ds * G, NSA_HPG, nh), axis=1)
        imp = _block_importance(p_grp, ns)
        blk = lax.broadcasted_iota(i32, (ds * G, ns), 1)
        work = jnp.where((blk == 0) | (blk == ns - 1), jnp.inf, imp)
        sel_ref[...] = _top_blocks(work, blk, min(N_SEL - 1, ns))
        m_ref[...] = jnp.full_like(m_ref, NEG_BIG)
        l_ref[...] = jnp.zeros_like(l_ref)
        acc_ref[...] = jnp.zeros_like(acc_ref)

    def online(s, v_list):
        m_prev = m_ref[...]
        m_new = jnp.maximum(m_prev, jnp.max(s, -1, keepdims=True))
        alpha = jnp.exp(m_prev - m_new)
        p = jnp.exp(s - m_new)
        l_ref[...] = alpha * l_ref[...] + jnp.sum(p, -1, keepdims=True)
        pb = p.astype(BF16)
        width = s.shape[1] // len(v_list)
        pv = _dot(pb[:, :width], v_list[0])
        for i in range(1, len(v_list)):
            pv += _dot(pb[:, i * width:(i + 1) * width], v_list[i])
        acc_ref[...] = alpha * acc_ref[...] + pv
        m_ref[...] = m_new

    nk = pages * PAGE_SIZE
    is_last = (j == n_steps - 1).astype(F32)
    s_parts = []
    for i in range(pages):
        s_i = _dot_nt(q, ks_refs[i][...].astype(BF16))
        if i == pages - 1:
            s_i = s_i + is_last * tile_ref[:, TILE_LAST:TILE_LAST + PAGE_SIZE]
        s_parts.append(s_i)
    s = jnp.concatenate(s_parts, axis=1)
    eb = lax.broadcasted_iota(i32, (ns, nk), 0)
    ek = (j * nk + lax.broadcasted_iota(i32, (ns, nk), 1)) >> SEL_SHIFT
    allow = _dot(sel_ref[...].astype(BF16), jnp.where(eb == ek, 1.0, 0.0).astype(BF16))
    mask = (allow - 1.0) * (-NEG_BIG)
    s = (s.reshape(ds * G, NSA_HPG, nk) + mask[:, None, :]).reshape(R, nk)
    online(s, [r[...].astype(BF16) for r in vs_refs])

    @pl.when(j == n_steps - 1)
    def _():
        new_b = new_ref[...].astype(BF16)
        u = lax.broadcasted_iota(i32, (R, ds), 1)
        nb = tile_ref[:, TILE_NEW:TILE_NEW + ds] + jnp.where(u <= tok, 0.0, NEG_INF)
        online(_dot_nt(q, new_b[0]) + nb, [new_b[1]])
        o_s = acc_ref[...] / l_ref[...]
        idx = lax.broadcasted_iota(i32, (R, win), 1)
        s1 = _dot_nt(q, skw_ref[...].astype(BF16)) + tile_ref[:, TILE_WIN:TILE_WIN + win]
        s1 = jnp.where(idx > tok + (win - WINDOW), s1, NEG_INF)
        s2 = _dot_nt(q, new_b[2]) + nb
        m_w = jnp.maximum(jnp.max(s1, -1, keepdims=True), jnp.max(s2, -1, keepdims=True))
        e1 = jnp.exp(s1 - m_w)
        e2 = jnp.exp(s2 - m_w)
        d_w = jnp.sum(e1, -1, keepdims=True) + jnp.sum(e2, -1, keepdims=True)
        o_w = (_dot(e1.astype(BF16), svw_ref[...].astype(BF16)) + _dot(e2.astype(BF16), new_b[3])) / d_w
        g = gate_ref[...]
        o_ref[...] = (g[0] * oc_ref[...] + g[1] * o_s + g[2] * o_w).astype(o_ref.dtype)


def _nsa_sample_attn(q_s, fs_k, fs_v, const, w2, state_kw, state_vw, new4, gates_s, tiles,
                     cache_ks, cache_vs, li, page_table, pages=8):
    DB, R, _ = q_s.shape
    ds = new4.shape[2]
    n_pages = page_table.shape[1]
    pages = min(pages, n_pages)
    past = n_pages * PAGE_SIZE
    nh = fs_k.shape[1]
    win = state_kw.shape[2]
    L, POOL = cache_ks.shape[:2]
    ks_view = cache_ks.reshape(L, POOL, PAGE_SIZE, LANES)
    vs_view = cache_vs.reshape(L, POOL, PAGE_SIZE, LANES)
    skw = state_kw.reshape(state_kw.shape[0], DB, win, LANES)
    svw = state_vw.reshape(state_vw.shape[0], DB, win, LANES)

    def page_spec(i):
        return pl.BlockSpec((None, None, PAGE_SIZE, LANES), lambda b, j, pt: (li, pt[b, j * pages + i], 0, 0))

    per_seq = lambda *shape: pl.BlockSpec((None,) + shape, lambda b, j, pt: (b,) + (0,) * len(shape))
    in_specs = [per_seq(R, LANES), per_seq(nh, 2 * LANES), per_seq(nh, 2 * LANES),
                pl.BlockSpec((2, 1, LANES), lambda b, j, pt: (0, 0, 0)),
                pl.BlockSpec((2, LANES, LANES), lambda b, j, pt: (0, 0, 0)),
                pl.BlockSpec((None, None, win, LANES), lambda b, j, pt: (li, b, 0, 0)),
                pl.BlockSpec((None, None, win, LANES), lambda b, j, pt: (li, b, 0, 0)),
                per_seq(4, ds, LANES), per_seq(3, R, 1),
                pl.BlockSpec(tiles.shape, lambda b, j, pt: (0, 0))]
    in_specs += [page_spec(i) for i in range(pages)] * 2
    return pl.pallas_call(
        functools.partial(_nsa_sample_kernel, pages=pages, ds=ds, past=past),
        grid_spec=pltpu.PrefetchScalarGridSpec(
            num_scalar_prefetch=1,
            grid=(DB, n_pages // pages),
            in_specs=in_specs,
            out_specs=per_seq(R, LANES),
            scratch_shapes=[pltpu.VMEM((ds * NSA_GROUPS, past // SEL_BLOCK), F32),
                            pltpu.VMEM((R, LANES), F32),
                            pltpu.VMEM((R, 1), F32), pltpu.VMEM((R, 1), F32), pltpu.VMEM((R, LANES), F32)]),
        out_shape=jax.ShapeDtypeStruct((DB, R, LANES), BF16),
        compiler_params=_cparams("parallel", "arbitrary"),
        name="nsa_sample_attn",
    )(page_table, q_s, fs_k, fs_v, const, w2, skw, svw, new4, gates_s, tiles,
      *([ks_view] * pages), *([vs_view] * pages))


def _nsa_bias_inputs(table, ds, past, win):
    i = np.arange(Q_BLOCK)[:, None]
    rel_key = i - np.arange(BAND)[None, :] + Q_BLOCK
    rel_cmp = i - CMP_STRIDE * (np.arange(CMP_BAND)[None, :] - CMP_BAND_LEAD) - (CMP_LEN - 1)
    sheet_p = np.concatenate([rel_key, rel_cmp, np.zeros((Q_BLOCK, LANES - CMP_BAND), np.int64)], 1)
    tp = _bias_tiles(table, _np_t5_bucket(sheet_p))
    hi, lo = _hi_lo(tp[:, :, :BAND])
    akey = jnp.concatenate([hi, lo], -1)
    hi, lo = _hi_lo(tp[:, :, BAND:BAND + CMP_BAND])
    aband = jnp.concatenate([hi, lo, jnp.zeros((NSA_HEADS, Q_BLOCK, LANES - 2 * CMP_BAND), BF16)], -1)
    t = np.arange(ds)[:, None]
    nh = past // CMP_STRIDE
    assert nh <= TILE_WIN - TILE_CMP and win <= TILE_LAST - TILE_WIN and ds <= SEL_BLOCK
    pad = lambda a, w: np.concatenate([a, np.zeros((ds, w - a.shape[1]), np.int64)], 1)
    rel_c = past + t - CMP_STRIDE * np.arange(nh)[None, :] - (CMP_LEN - 1)
    rel_w = t + win - np.arange(win)[None, :]
    rel_l = t + PAGE_SIZE - np.arange(PAGE_SIZE)[None, :]
    rel_n = t - np.arange(ds)[None, :]
    sheet_s = np.concatenate([pad(rel_c, TILE_WIN - TILE_CMP), pad(rel_w, TILE_LAST - TILE_WIN),
                              pad(rel_l, TILE_NEW - TILE_LAST), pad(rel_n, LANES)], 1)
    ts = _bias_tiles(table, _np_t5_bucket(sheet_s))
    tiles = jnp.transpose(ts, (1, 0, 2)).reshape(ds * NSA_HEADS, -1)
    return akey, aband, tiles


def _nsa_layer(X, dims, li, bias_inputs, cache_kc, cache_vc, cache_ks, cache_vs, state_kw, state_vw, page_table,
               w_in, pe_k, w1_k, w2_k, pe_v, w1_v, w2_v, w_o, g, b):
    B, S, DB, DS = dims
    MP = B * S
    M, D = X.shape
    H = NSA_HEADS
    akey, aband, tiles = bias_inputs
    qw = H * NSA_DH
    kvw = 6 * NSA_GROUPS * NSA_DH
    wq = _pad_heads_to_group_lanes(w_in[:, :qw].reshape(D, H, NSA_DH), 1).reshape(D, H * LANES)
    w_ext = jnp.concatenate([wq, w_in[:, qw:], jnp.zeros((D, LANES - (w_in.shape[1] - qw - kvw)), F32)], 1)
    q, kv, kvb, gates = _nsa_in(X, w_ext.astype(BF16))
    wk_big, pe2_k, w2k = _compress_weights(pe_k, w1_k, w2_k)
    wv_big, pe2_v, w2v = _compress_weights(pe_v, w1_v, w2_v)
    w2 = jnp.stack([w2k, w2v])
    nh = S // CMP_STRIDE
    cmp_kv, const = _nsa_compress(kvb[:2, :MP].reshape(2, B, nh, CMP_STRIDE * LANES),
                                  jnp.stack([wk_big, wv_big]), jnp.stack([pe2_k, pe2_v]), w2)
    o = _nsa_prompt_attn(q, cmp_kv, kvb, gates, akey, aband, B, S, M)
    q_s = jnp.transpose(q[:, MP:].reshape(H, DB, DS, LANES), (1, 2, 0, 3)).reshape(DB, DS * H, LANES)
    fs_k = _nsa_fs_sample(cache_kc, li, page_table, wk_big)
    fs_v = _nsa_fs_sample(cache_vc, li, page_table, wv_big)
    new4 = jnp.transpose(kv[2:, MP:].reshape(4, DB, DS, LANES), (1, 0, 2, 3))
    gates_s = jnp.transpose(gates[MP:, :3 * H].reshape(DB, DS, 3, H), (0, 2, 1, 3)).reshape(DB, 3, DS * H, 1)
    o_s = _nsa_sample_attn(q_s, fs_k, fs_v, const[:, 0, :1], w2, state_kw, state_vw, new4, gates_s, tiles,
                           cache_ks, cache_vs, li, page_table)
    o = lax.dynamic_update_slice(o, o_s.reshape(DB * DS, H * LANES), (MP, 0))
    wo = _pad_heads_to_group_lanes(w_o.reshape(H, NSA_DH, D), 0).reshape(H * LANES, D)
    X = _mm_ln(o, wo.astype(BF16), X, g, b, name="nsa_out")
    split = lambda a: (a[:MP].reshape(B, S, NSA_GROUPS, NSA_DH), a[MP:].reshape(DB, DS, NSA_GROUPS, NSA_DH))
    pieces = [split(kv[j]) for j in range(6)]
    keep = min(WINDOW, S)
    outs_p = [p for p, _ in pieces[:4]] + [pieces[4][0][:, S - keep:], pieces[5][0][:, S - keep:]]
    outs_s = [s for _, s in pieces[:4]] + [jnp.concatenate([state_kw[li], pieces[4][1]], 1)[:, DS:],
                                           jnp.concatenate([state_vw[li], pieces[5][1]], 1)[:, DS:]]
    return X, outs_p, outs_s


def kernel(x_prompt, x_sample, cache_mla_ckv, cache_mla_kpe, cache_nsa_kc, cache_nsa_vc, cache_nsa_ks, cache_nsa_vs, state_nsa_kw, state_nsa_vw, cache_mem_k, cache_mem_v, page_table, mem_prompt, ln_g, ln_b, mla_w_down, mla_q_norm, mla_kv_norm, mla_w_uq, mla_w_uk, mla_w_uv, mla_w_o, nsa_w_in, nsa_cmp_pe_k, nsa_cmp_w1_k, nsa_cmp_w2_k, nsa_cmp_pe_v, nsa_cmp_w1_v, nsa_cmp_w2_v, nsa_w_o, rel_bias, xa_w_q, xa_w_kv, xa_w_o, ffn_w_gu, ffn_w_down, moe_router, moe_w_gu, moe_w_down):
    B, S, D = x_prompt.shape
    DB, DS, _ = x_sample.shape
    dims = (B, S, DB, DS)
    MP, MS = B * S, DB * DS
    past = page_table.shape[1] * PAGE_SIZE
    n_mem = mem_prompt.shape[1]
    depth = ln_g.shape[0]
    HD = MEM_HEADS * MEM_DH

    X = jnp.concatenate([x_prompt.reshape(MP, D), x_sample.reshape(MS, D)], 0)
    pos = jnp.concatenate([jnp.tile(jnp.arange(S, dtype=jnp.int32), B),
                           jnp.tile(past + jnp.arange(DS, dtype=jnp.int32), DB)])
    cos, sin = _rope_tables(pos)
    mem_rows = mem_prompt.reshape(B * n_mem, D)
    bias_inputs = _nsa_bias_inputs(rel_bias, DS, past, state_nsa_kw.shape[2])

    outs_p = {k: [] for k in ("ckv", "kpe", "kc", "vc", "ks", "vs", "kw", "vw", "mk", "mv")}
    outs_s = {k: [] for k in ("ckv", "kpe", "kc", "vc", "ks", "vs", "kw", "vw")}
    for i in range(depth):
        li = i // 2
        if i % 2 == 0:
            X, ckv, kpe = _mla_layer(X, dims, cos, sin, cache_mla_ckv, cache_mla_kpe, li, page_table,
                                     mla_w_down[li], mla_q_norm[li], mla_kv_norm[li], mla_w_uq[li],
                                     mla_w_uk[li], mla_w_uv[li], mla_w_o[li], ln_g[i, 0], ln_b[i, 0])
            outs_p["ckv"].append(ckv[:MP].reshape(B, S, -1))
            outs_p["kpe"].append(kpe[:MP].reshape(B, S, -1))
            outs_s["ckv"].append(ckv[MP:].reshape(DB, DS, -1))
            outs_s["kpe"].append(kpe[MP:].reshape(DB, DS, -1))
        else:
            X, o_p, o_s = _nsa_layer(X, dims, li, bias_inputs, cache_nsa_kc, cache_nsa_vc, cache_nsa_ks,
                                     cache_nsa_vs, state_nsa_kw, state_nsa_vw, page_table,
                                     nsa_w_in[li], nsa_cmp_pe_k[li], nsa_cmp_w1_k[li], nsa_cmp_w2_k[li],
                                     nsa_cmp_pe_v[li], nsa_cmp_w1_v[li], nsa_cmp_w2_v[li], nsa_w_o[li],
                                     ln_g[i, 0], ln_b[i, 0])
            for k, v in zip(("kc", "vc", "ks", "vs", "kw", "vw"), o_p):
                outs_p[k].append(v)
            for k, v in zip(("kc", "vc", "ks", "vs", "kw", "vw"), o_s):
                outs_s[k].append(v)
        kv = _mm(mem_rows, xa_w_kv[i].astype(BF16), name="mem_kv").reshape(B, n_mem, 2, MEM_HEADS, MEM_DH)
        mk, mv = kv[:, :, 0], kv[:, :, 1]
        outs_p["mk"].append(mk)
        outs_p["mv"].append(mv)
        q = _mm(X, xa_w_q[i].astype(BF16), out_dtype=BF16, scale=MEM_SCALE, name="xattn_q")
        o_p = _xattn(q[:MP].reshape(B, S, HD), mk.reshape(B, n_mem, HD), mv.reshape(B, n_mem, HD),
                     rows=min(ROW_TILE, S), group=1)
        o_s = _xattn(q[MP:].reshape(DB, DS, HD), cache_mem_k[i].reshape(DB, n_mem, HD),
                     cache_mem_v[i].reshape(DB, n_mem, HD), rows=DS, group=min(8, DB))
        o = jnp.concatenate([o_p.reshape(MP, HD), o_s.reshape(MS, HD)], 0)
        X = _mm_ln(o, xa_w_o[i].astype(BF16), X, ln_g[i, 1], ln_b[i, 1], name="xattn_out")
        if i % 2 == 0:
            X = _ffn_ln(X, ffn_w_gu[li][None].astype(BF16), ffn_w_down[li][None].astype(BF16),
                        ln_g[i, 2], ln_b[i, 2])
        else:
            comb = _router(X, moe_router[li])
            X = _ffn_ln(X, moe_w_gu[li].astype(BF16), moe_w_down[li].astype(BF16),
                        ln_g[i, 2], ln_b[i, 2], comb=comb)

    st = jnp.stack
    return (X[:MP].reshape(B, S, D), X[MP:].reshape(DB, DS, D),
            st(outs_p["ckv"]), st(outs_p["kpe"]), st(outs_p["kc"]), st(outs_p["vc"]),
            st(outs_p["ks"]), st(outs_p["vs"]), st(outs_p["kw"]), st(outs_p["vw"]),
            st(outs_p["mk"]), st(outs_p["mv"]),
            st(outs_s["ckv"]), st(outs_s["kpe"]), st(outs_s["kc"]), st(outs_s["vc"]),
            st(outs_s["ks"]), st(outs_s["vs"]), st(outs_s["kw"]), st(outs_s["vw"]))
```

```python
import functools
import math

import jax
import jax.numpy as jnp
import numpy as np
from jax import lax
from jax.experimental import pallas as pl
from jax.experimental.pallas import tpu as pltpu

F32 = jnp.float32
BF16 = jnp.bfloat16

DEPTH = 4
PAGE_SIZE = 128
MLA_HEADS = 16
MLA_Q_RANK = 384
MLA_KV_RANK = 256
MLA_NOPE = 128
MLA_ROPE = 64
MLA_VDIM = 128
MLA_SCALE = (MLA_NOPE + MLA_ROPE) ** -0.5
ROPE_THETA = 10000.0
NSA_HEADS = 16
NSA_GROUPS = 2
NSA_HPG = NSA_HEADS // NSA_GROUPS
NSA_DH = 64
NSA_SCALE = NSA_DH ** -0.5
CMP_LEN = 32
CMP_STRIDE = 16
SEL_BLOCK = 64
N_SEL = 16
WINDOW = 512
N_BUCKETS = 32
MAX_DISTANCE = 128
MEM_HEADS = 4
MEM_DH = 128
MEM_SCALE = MEM_DH ** -0.5
N_EXPERTS = 8
ALPHA = (2 * DEPTH) ** 0.25
LN_EPS = 1e-5
RMS_EPS = 1e-6
Q_BLOCK = 128

LANES = 128
VMEM_LIMIT = 56 * 1024 * 1024
ROW_TILE = 512
NEG_INF = float("-inf")
NEG_BIG = -1e30

ATTN_HEAD_PARTS = 8
MLA_CHUNK = 1024

SEL_CHUNK = 512
BAND = 2 * Q_BLOCK
CMP_BAND = 16
CMP_BAND_LEAD = 9
SEL_SHIFT = SEL_BLOCK.bit_length() - 1
HEAD_SHIFT = NSA_HEADS.bit_length() - 1
TILE_CMP, TILE_WIN, TILE_LAST, TILE_NEW = 0, 4 * LANES, 8 * LANES, 9 * LANES


def _cparams(*sem):
    return pltpu.CompilerParams(dimension_semantics=tuple(sem), vmem_limit_bytes=VMEM_LIMIT)


def _dot(a, b):
    return jnp.dot(a, b, preferred_element_type=F32)


def _dot_nt(a, b):
    return lax.dot_general(a, b, (((1,), (1,)), ((), ())), preferred_element_type=F32)


def _layer_norm(h, g, b):
    mu = jnp.mean(h, -1, keepdims=True)
    d = h - mu
    var = jnp.mean(d * d, -1, keepdims=True)
    return d * lax.rsqrt(var + LN_EPS) * g + b


def _rms_norm(x, g):
    return x * lax.rsqrt(jnp.mean(x * x, -1, keepdims=True) + RMS_EPS) * g


def _mm_kernel(x_ref, w_ref, o_ref, *, scale):
    acc = _dot(x_ref[...].astype(BF16), w_ref[...])
    if scale is not None:
        acc = acc * scale
    o_ref[...] = acc.astype(o_ref.dtype)


def _mm(x, w, out_dtype=F32, scale=None, tm=ROW_TILE, name="mm"):
    M, K = x.shape
    N = w.shape[1]
    tm = math.gcd(M, tm)
    return pl.pallas_call(
        functools.partial(_mm_kernel, scale=scale),
        grid=(M // tm,),
        in_specs=[pl.BlockSpec((tm, K), lambda i: (i, 0)),
                  pl.BlockSpec((K, N), lambda i: (0, 0))],
        out_specs=pl.BlockSpec((tm, N), lambda i: (i, 0)),
        out_shape=jax.ShapeDtypeStruct((M, N), out_dtype),
        compiler_params=_cparams("parallel"),
        name=name,
    )(x, w)


def _mm_ln_kernel(a_ref, w_ref, res_ref, g_ref, b_ref, o_ref):
    acc = _dot(a_ref[...].astype(BF16), w_ref[...])
    o_ref[...] = _layer_norm(ALPHA * res_ref[...] + acc, g_ref[...], b_ref[...])


def _mm_ln(a, w, res, g, b, tm=ROW_TILE, name="mm_ln"):
    M, K = a.shape
    D = w.shape[1]
    tm = math.gcd(M, tm)
    return pl.pallas_call(
        _mm_ln_kernel,
        grid=(M // tm,),
        in_specs=[pl.BlockSpec((tm, K), lambda i: (i, 0)),
                  pl.BlockSpec((K, D), lambda i: (0, 0)),
                  pl.BlockSpec((tm, D), lambda i: (i, 0)),
                  pl.BlockSpec((1, D), lambda i: (0, 0)),
                  pl.BlockSpec((1, D), lambda i: (0, 0))],
        out_specs=pl.BlockSpec((tm, D), lambda i: (i, 0)),
        out_shape=jax.ShapeDtypeStruct((M, D), F32),
        compiler_params=_cparams("parallel"),
        name=name,
    )(a, w, res, g.reshape(1, D), b.reshape(1, D))


def _router_kernel(x_ref, r_ref, o_ref):
    logits = jnp.dot(x_ref[...], r_ref[...], preferred_element_type=F32,
                     precision=lax.Precision.HIGHEST)
    lane = lax.broadcasted_iota(jnp.int32, logits.shape, 1)
    logits = jnp.where(lane < N_EXPERTS, logits, NEG_INF)
    v1 = jnp.max(logits, -1, keepdims=True)
    i1 = jnp.min(jnp.where(logits == v1, lane, LANES), -1, keepdims=True)
    rest = jnp.where(lane == i1, NEG_INF, logits)
    v2 = jnp.max(rest, -1, keepdims=True)
    i2 = jnp.min(jnp.where(rest == v2, lane, LANES), -1, keepdims=True)
    e2 = jnp.exp(v2 - v1)
    w1 = 1.0 / (1.0 + e2)
    w2 = e2 / (1.0 + e2)
    o_ref[...] = jnp.where(lane == i1, w1, 0.0) + jnp.where(lane == i2, w2, 0.0)


def _router(x, router, tm=ROW_TILE):
    M, D = x.shape
    tm = math.gcd(M, tm)
    r = jnp.pad(router, ((0, 0), (0, LANES - router.shape[1])))
    return pl.pallas_call(
        _router_kernel,
        grid=(M // tm,),
        in_specs=[pl.BlockSpec((tm, D), lambda i: (i, 0)),
                  pl.BlockSpec((D, LANES), lambda i: (0, 0))],
        out_specs=pl.BlockSpec((tm, LANES), lambda i: (i, 0)),
        out_shape=jax.ShapeDtypeStruct((M, LANES), F32),
        compiler_params=_cparams("parallel"),
        name="moe_router",
    )(x, r)


def _ffn_kernel(*refs, n_e, n_f, routed):
    if routed:
        x_ref, comb_ref, wg_ref, wu_ref, wd_ref, g_ref, b_ref, o_ref, acc_ref, xb_ref = refs
    else:
        x_ref, wg_ref, wu_ref, wd_ref, g_ref, b_ref, o_ref, acc_ref, xb_ref = refs
    e = pl.program_id(1)
    f = pl.program_id(2)

    @pl.when((e == 0) & (f == 0))
    def _():
        acc_ref[...] = jnp.zeros_like(acc_ref)
        xb_ref[...] = x_ref[...].astype(BF16)

    xb = xb_ref[...]
    gate = _dot(xb, wg_ref[...])
    up = _dot(xb, wu_ref[...])
    act = gate * jax.nn.sigmoid(gate) * up
    if routed:
        comb = comb_ref[...]
        lane = lax.broadcasted_iota(jnp.int32, comb.shape, 1)
        act = act * jnp.sum(jnp.where(lane == e, comb, 0.0), -1, keepdims=True)
    acc_ref[...] += _dot(act.astype(BF16), wd_ref[...])

    @pl.when((e == n_e - 1) & (f == n_f - 1))
    def _():
        o_ref[...] = _layer_norm(ALPHA * x_ref[...] + acc_ref[...], g_ref[...], b_ref[...])


def _ffn_ln(x, w_gu, w_down, g, b, comb=None, tm=ROW_TILE, tf=1408):
    M, D = x.shape
    tm = math.gcd(M, tm)
    E, F = w_down.shape[0], w_down.shape[1]
    n_f = F // tf
    routed = comb is not None
    in_specs = [pl.BlockSpec((tm, D), lambda i, e, f: (i, 0))]
    args = [x]
    if routed:
        in_specs.append(pl.BlockSpec((tm, LANES), lambda i, e, f: (i, 0)))
        args.append(comb)
    in_specs += [pl.BlockSpec((None, D, tf), lambda i, e, f: (e, 0, f)),
                 pl.BlockSpec((None, D, tf), lambda i, e, f: (e, 0, f + n_f)),
                 pl.BlockSpec((None, tf, D), lambda i, e, f: (e, f, 0)),
                 pl.BlockSpec((1, D), lambda i, e, f: (0, 0)),
                 pl.BlockSpec((1, D), lambda i, e, f: (0, 0))]
    args += [w_gu, w_gu, w_down, g.reshape(1, D), b.reshape(1, D)]
    return pl.pallas_call(
        functools.partial(_ffn_kernel, n_e=E, n_f=n_f, routed=routed),
        grid=(M // tm, E, n_f),
        in_specs=in_specs,
        out_specs=pl.BlockSpec((tm, D), lambda i, e, f: (i, 0)),
        out_shape=jax.ShapeDtypeStruct((M, D), F32),
        scratch_shapes=[pltpu.VMEM((tm, D), F32), pltpu.VMEM((tm, D), BF16)],
        compiler_params=_cparams("parallel", "arbitrary", "arbitrary"),
        name="ffn_ln",
    )(*args)


def _xattn_kernel(q_ref, k_ref, v_ref, o_ref):
    for gi in range(q_ref.shape[0]):
        for h in range(MEM_HEADS):
            sl = slice(h * MEM_DH, (h + 1) * MEM_DH)
            qh = q_ref[gi, :, sl]
            kh = k_ref[gi, :, sl].astype(BF16)
            vh = v_ref[gi, :, sl].astype(BF16)
            s = _dot_nt(qh, kh)
            m = jnp.max(s, -1, keepdims=True)
            p = jnp.exp(s - m)
            p = p / jnp.sum(p, -1, keepdims=True)
            o_ref[gi, :, sl] = _dot(p.astype(BF16), vh).astype(o_ref.dtype)


def _xattn(q, k, v, rows, group):
    NB, R, HD = q.shape
    NM = k.shape[1]
    return pl.pallas_call(
        _xattn_kernel,
        grid=(NB // group, R // rows),
        in_specs=[pl.BlockSpec((group, rows, HD), lambda b, i: (b, i, 0)),
                  pl.BlockSpec((group, NM, HD), lambda b, i: (b, 0, 0)),
                  pl.BlockSpec((group, NM, HD), lambda b, i: (b, 0, 0))],
        out_specs=pl.BlockSpec((group, rows, HD), lambda b, i: (b, i, 0)),
        out_shape=jax.ShapeDtypeStruct((NB, R, HD), BF16),
        compiler_params=_cparams("parallel", "parallel"),
        name="xattn",
    )(q, k, v)


def _rope_tables(pos):
    half = MLA_ROPE // 2
    inv = ROPE_THETA ** (-jnp.arange(half, dtype=F32) / half)
    ang = pos.astype(F32)[:, None] * inv[None, :]
    c, s = jnp.cos(ang), jnp.sin(ang)
    return jnp.concatenate([c, c], -1), jnp.concatenate([-s, s], -1)


def _swap_halves(w):
    half = w.shape[-1] // 2
    return jnp.concatenate([w[..., half:], w[..., :half]], -1)


def _mla_down_kernel(x_ref, w_ref, qn_ref, kvn_ref, cos_ref, sin_ref,
                     cq_ref, ckv_ref, kpe_ref, ckvb_ref, kpeb_ref):
    down = _dot(x_ref[...].astype(BF16), w_ref[...])
    cq_ref[...] = _rms_norm(down[:, :MLA_Q_RANK], qn_ref[...]).astype(BF16)
    ckv = _rms_norm(down[:, MLA_Q_RANK:MLA_Q_RANK + MLA_KV_RANK], kvn_ref[...])
    ckv_ref[...] = ckv
    ckvb_ref[...] = ckv.astype(BF16)
    ra = down[:, 640:640 + MLA_ROPE]
    rb = down[:, 768:768 + MLA_ROPE]
    kpe = ra * cos_ref[...] + rb * sin_ref[...]
    kpe_ref[...] = kpe
    kpeb_ref[...] = kpe.astype(BF16)


def _mla_down(x, w_ext, q_norm, kv_norm, cos, sin, tm=ROW_TILE):
    M, D = x.shape
    tm = math.gcd(M, tm)
    NW = w_ext.shape[1]
    row = lambda n: pl.BlockSpec((tm, n), lambda i: (i, 0))
    full = lambda a, b: pl.BlockSpec((a, b), lambda i: (0, 0))
    return pl.pallas_call(
        _mla_down_kernel,
        grid=(M // tm,),
        in_specs=[row(D), full(D, NW), full(1, MLA_Q_RANK), full(1, MLA_KV_RANK),
                  row(MLA_ROPE), row(MLA_ROPE)],
        out_specs=[row(MLA_Q_RANK), row(MLA_KV_RANK), row(MLA_ROPE), row(MLA_KV_RANK), row(MLA_ROPE)],
        out_shape=[jax.ShapeDtypeStruct((M, MLA_Q_RANK), BF16),
                   jax.ShapeDtypeStruct((M, MLA_KV_RANK), F32),
                   jax.ShapeDtypeStruct((M, MLA_ROPE), F32),
                   jax.ShapeDtypeStruct((M, MLA_KV_RANK), BF16),
                   jax.ShapeDtypeStruct((M, MLA_ROPE), BF16)],
        compiler_params=_cparams("parallel"),
        name="mla_down",
    )(x, w_ext, q_norm.reshape(1, -1), kv_norm.reshape(1, -1), cos, sin)


def _mla_q_kernel(cq_ref, wq_ref, wk_ref, cos_ref, sin_ref, ql_ref, qp_ref):
    q = _dot(cq_ref[...], wq_ref[...])
    nope = q[:, :MLA_NOPE].astype(BF16)
    ra = q[:, MLA_NOPE:MLA_NOPE + MLA_ROPE]
    rb = q[:, MLA_NOPE + LANES:MLA_NOPE + LANES + MLA_ROPE]
    qp_ref[...] = ((ra * cos_ref[...] + rb * sin_ref[...]) * MLA_SCALE).astype(BF16)
    ql_ref[...] = (_dot_nt(nope, wk_ref[...]) * MLA_SCALE).astype(BF16)


def _mla_q(cq, wq_ext, wk, cos, sin, tm=ROW_TILE):
    M = cq.shape[0]
    tm = math.gcd(M, tm)
    H = wq_ext.shape[0]
    return pl.pallas_call(
        _mla_q_kernel,
        grid=(M // tm, H),
        in_specs=[pl.BlockSpec((tm, MLA_Q_RANK), lambda i, h: (i, 0)),
                  pl.BlockSpec((None, MLA_Q_RANK, MLA_NOPE + 2 * LANES), lambda i, h: (h, 0, 0)),
                  pl.BlockSpec((None, MLA_KV_RANK, MLA_NOPE), lambda i, h: (h, 0, 0)),
                  pl.BlockSpec((tm, MLA_ROPE), lambda i, h: (i, 0)),
                  pl.BlockSpec((tm, MLA_ROPE), lambda i, h: (i, 0))],
        out_specs=[pl.BlockSpec((None, tm, MLA_KV_RANK), lambda i, h: (h, i, 0)),
                   pl.BlockSpec((None, tm, MLA_ROPE), lambda i, h: (h, i, 0))],
        out_shape=[jax.ShapeDtypeStruct((H, M, MLA_KV_RANK), BF16),
                   jax.ShapeDtypeStruct((H, M, MLA_ROPE), BF16)],
        compiler_params=_cparams("parallel", "parallel"),
        name="mla_q",
    )(cq, wq_ext, wk, cos, sin)


def _online_softmax_step(s, v, m_ref, l_ref, acc_ref):
    m_prev = m_ref[...]
    m_new = jnp.maximum(m_prev, jnp.max(s, -1, keepdims=True))
    alpha = jnp.exp(m_prev - m_new)
    p = jnp.exp(s - m_new)
    l_ref[...] = alpha * l_ref[...] + jnp.sum(p, -1, keepdims=True)
    acc_ref[...] = alpha * acc_ref[...] + _dot(p.astype(BF16), v)
    m_ref[...] = m_new


def _mla_prompt_attn_kernel(ql_ref, qp_ref, ckv_ref, kpe_ref, o_ref, m_ref, l_ref, acc_ref, *, tk):
    H, TQ = ql_ref.shape[0], ql_ref.shape[1]
    q0 = pl.program_id(1) * TQ
    m_ref[...] = jnp.full_like(m_ref, NEG_INF)
    l_ref[...] = jnp.zeros_like(l_ref)
    acc_ref[...] = jnp.zeros_like(acc_ref)
    n_full = q0 // tk
    hp = H // ATTN_HEAD_PARTS
    rp = hp * TQ

    def chunk(kb, masked):
        k0 = pl.multiple_of(kb * tk, tk)
        kc = ckv_ref[pl.ds(k0, tk), :]
        kp = kpe_ref[pl.ds(k0, tk), :]
        if masked:
            qpos = q0 + lax.broadcasted_iota(jnp.int32, (TQ, tk), 0)
            kpos = k0 + lax.broadcasted_iota(jnp.int32, (TQ, tk), 1)
            visible = (kpos <= qpos)[None]
        for part in range(ATTN_HEAD_PARTS):
            heads = slice(part * hp, (part + 1) * hp)
            rows = slice(part * rp, (part + 1) * rp)
            s = (_dot_nt(ql_ref[heads].reshape(rp, MLA_KV_RANK), kc)
                 + _dot_nt(qp_ref[heads].reshape(rp, MLA_ROPE), kp))
            if masked:
                s = jnp.where(visible, s.reshape(hp, TQ, tk), NEG_INF).reshape(rp, tk)
            _online_softmax_step(s, kc, m_ref.at[rows], l_ref.at[rows], acc_ref.at[rows])

    def body(kb, carry):
        chunk(kb, False)
        return carry

    lax.fori_loop(0, n_full, body, 0)
    chunk(n_full, True)
    o_ref[...] = (acc_ref[...] / l_ref[...]).reshape(H, TQ, MLA_KV_RANK).astype(o_ref.dtype)


def _mla_prompt_attn(q_lat, q_pe, ckv_b, kpe_b, B, S, m_total, tk=MLA_CHUNK):
    H = q_lat.shape[0]
    tk = min(tk, S)
    nq = S // Q_BLOCK
    ckv3 = ckv_b[:B * S].reshape(B, S, MLA_KV_RANK)
    kpe3 = kpe_b[:B * S].reshape(B, S, MLA_ROPE)
    return pl.pallas_call(
        functools.partial(_mla_prompt_attn_kernel, tk=tk),
        grid=(B, nq),
        in_specs=[pl.BlockSpec((H, Q_BLOCK, MLA_KV_RANK), lambda b, i: (0, b * nq + i, 0)),
                  pl.BlockSpec((H, Q_BLOCK, MLA_ROPE), lambda b, i: (0, b * nq + i, 0)),
                  pl.BlockSpec((None, S, MLA_KV_RANK), lambda b, i: (b, 0, 0)),
                  pl.BlockSpec((None, S, MLA_ROPE), lambda b, i: (b, 0, 0))],
        out_specs=pl.BlockSpec((H, Q_BLOCK, MLA_KV_RANK), lambda b, i: (0, b * nq + i, 0)),
        out_shape=jax.ShapeDtypeStruct((H, m_total, MLA_KV_RANK), BF16),
        scratch_shapes=[pltpu.VMEM((H * Q_BLOCK, 1), F32), pltpu.VMEM((H * Q_BLOCK, 1), F32),
                        pltpu.VMEM((H * Q_BLOCK, MLA_KV_RANK), F32)],
        compiler_params=_cparams("parallel", "parallel"),
        name="mla_prompt_attn",
    )(q_lat, q_pe, ckv3, kpe3)


def _mla_sample_attn_kernel(pt_ref, ql_ref, qp_ref, cn_ref, kn_ref, *rest, pages, ds):
    ckv_refs = rest[:pages]
    kpe_refs = rest[pages:2 * pages]
    o_ref, m_ref, l_ref, acc_ref = rest[2 * pages:]
    j = pl.program_id(1)

    @pl.when(j == 0)
    def _():
        m_ref[...] = jnp.full_like(m_ref, NEG_INF)
        l_ref[...] = jnp.zeros_like(l_ref)
        acc_ref[...] = jnp.zeros_like(acc_ref)

    ql = ql_ref[...]
    qp = qp_ref[...]
    kcs = [r[...].astype(BF16) for r in ckv_refs]
    s = jnp.concatenate([_dot_nt(ql, kc) + _dot(qp, kr[...].astype(BF16))
                         for kc, kr in zip(kcs, kpe_refs)], axis=1)
    m_prev = m_ref[...]
    m_new = jnp.maximum(m_prev, jnp.max(s, -1, keepdims=True))
    alpha = jnp.exp(m_prev - m_new)
    p = jnp.exp(s - m_new).astype(BF16)
    l_ref[...] = alpha * l_ref[...] + jnp.sum(p.astype(F32), -1, keepdims=True)
    pv = _dot(p[:, :PAGE_SIZE], kcs[0])
    for i in range(1, pages):
        pv += _dot(p[:, i * PAGE_SIZE:(i + 1) * PAGE_SIZE], kcs[i])
    acc_ref[...] = alpha * acc_ref[...] + pv
    m_ref[...] = m_new

    @pl.when(j == pl.num_programs(1) - 1)
    def _():
        cn = cn_ref[...].astype(BF16)
        s_new = _dot_nt(ql, cn) + _dot_nt(qp, kn_ref[...].astype(BF16))
        t = lax.broadcasted_iota(jnp.int32, s_new.shape, 0) % ds
        u = lax.broadcasted_iota(jnp.int32, s_new.shape, 1)
        s_new = jnp.where(u <= t, s_new, NEG_INF)
        _online_softmax_step(s_new, cn, m_ref, l_ref, acc_ref)
        o_ref[...] = (acc_ref[...] / l_ref[...]).astype(o_ref.dtype)


def _mla_sample_attn(q_lat_s, q_pe_s, cnew, knew, cache_ckv, cache_kpe, li, page_table, pages=8):
    DB, R, _ = q_lat_s.shape
    ds = cnew.shape[1]
    n_pages = page_table.shape[1]
    pages = min(pages, n_pages)

    def page_spec(rows, width, i):
        return pl.BlockSpec((None, None, rows, width),
                            lambda b, j, pt: (li, pt[b, j * pages + i], 0, 0))

    kpe_t = jnp.transpose(cache_kpe, (0, 1, 3, 2))
    in_specs = [pl.BlockSpec((None, R, MLA_KV_RANK), lambda b, j, pt: (b, 0, 0)),
                pl.BlockSpec((None, R, MLA_ROPE), lambda b, j, pt: (b, 0, 0)),
                pl.BlockSpec((None, ds, MLA_KV_RANK), lambda b, j, pt: (b, 0, 0)),
                pl.BlockSpec((None, ds, MLA_ROPE), lambda b, j, pt: (b, 0, 0))]
    in_specs += [page_spec(PAGE_SIZE, MLA_KV_RANK, i) for i in range(pages)]
    in_specs += [page_spec(MLA_ROPE, PAGE_SIZE, i) for i in range(pages)]
    return pl.pallas_call(
        functools.partial(_mla_sample_attn_kernel, pages=pages, ds=ds),
        grid_spec=pltpu.PrefetchScalarGridSpec(
            num_scalar_prefetch=1,
            grid=(DB, n_pages // pages),
            in_specs=in_specs,
            out_specs=pl.BlockSpec((None, R, MLA_KV_RANK), lambda b, j, pt: (b, 0, 0)),
            scratch_shapes=[pltpu.VMEM((R, 1), F32), pltpu.VMEM((R, 1), F32),
                            pltpu.VMEM((R, MLA_KV_RANK), F32)]),
        out_shape=jax.ShapeDtypeStruct((DB, R, MLA_KV_RANK), BF16),
        compiler_params=_cparams("parallel", "arbitrary"),
        name="mla_sample_attn",
    )(page_table, q_lat_s, q_pe_s, cnew, knew, *([cache_ckv] * pages), *([kpe_t] * pages))


def _mla_out_kernel(ol_ref, wuv_ref, wo_ref, res_ref, g_ref, b_ref, o_ref):
    acc = None
    for h in range(ol_ref.shape[0]):
        oh = _dot(ol_ref[h], wuv_ref[h]).astype(BF16)
        part = _dot(oh, wo_ref[h * MLA_VDIM:(h + 1) * MLA_VDIM, :])
        acc = part if acc is None else acc + part
    o_ref[...] = _layer_norm(ALPHA * res_ref[...] + acc, g_ref[...], b_ref[...])


def _mla_out(o_lat, w_uv, w_o, res, g, b, tm=ROW_TILE):
    H, M, R = o_lat.shape
    tm = math.gcd(M, tm)
    D = w_o.shape[1]
    return pl.pallas_call(
        _mla_out_kernel,
        grid=(M // tm,),
        in_specs=[pl.BlockSpec((H, tm, R), lambda i: (0, i, 0)),
                  pl.BlockSpec((H, R, MLA_VDIM), lambda i: (0, 0, 0)),
                  pl.BlockSpec((H * MLA_VDIM, D), lambda i: (0, 0)),
                  pl.BlockSpec((tm, D), lambda i: (i, 0)),
                  pl.BlockSpec((1, D), lambda i: (0, 0)),
                  pl.BlockSpec((1, D), lambda i: (0, 0))],
        out_specs=pl.BlockSpec((tm, D), lambda i: (i, 0)),
        out_shape=jax.ShapeDtypeStruct((M, D), F32),
        compiler_params=_cparams("parallel"),
        name="mla_out",
    )(o_lat, w_uv, w_o, res, g.reshape(1, D), b.reshape(1, D))


def _mla_layer(X, dims, cos, sin, cache_ckv, cache_kpe, li, page_table,
               w_down, q_norm, kv_norm, w_uq, w_uk, w_uv, w_o, g, b):
    B, S, DB, DS = dims
    MP = B * S
    M, D = X.shape
    H = MLA_HEADS
    rope_w = w_down[:, MLA_Q_RANK + MLA_KV_RANK:]
    zpad = jnp.zeros((D, LANES - MLA_ROPE), F32)
    w_ext = jnp.concatenate([w_down, zpad, _swap_halves(rope_w), zpad], 1).astype(BF16)
    cq, ckv, kpe, ckv_b, kpe_b = _mla_down(X, w_ext, q_norm, kv_norm, cos, sin)
    wq = w_uq.reshape(MLA_Q_RANK, H, MLA_NOPE + MLA_ROPE)
    hpad = jnp.zeros((MLA_Q_RANK, H, LANES - MLA_ROPE), F32)
    wq_ext = jnp.concatenate([wq, hpad, _swap_halves(wq[..., MLA_NOPE:]), hpad], -1)
    wq_ext = jnp.transpose(wq_ext, (1, 0, 2)).astype(BF16)
    wk = jnp.transpose(w_uk, (1, 0, 2)).astype(BF16)
    q_lat, q_pe = _mla_q(cq, wq_ext, wk, cos, sin)
    o_lat = _mla_prompt_attn(q_lat, q_pe, ckv_b, kpe_b, B, S, M)
    to_seq = lambda a: jnp.transpose(a[:, MP:].reshape(H, DB, DS, -1), (1, 0, 2, 3)).reshape(DB, H * DS, -1)
    o_s = _mla_sample_attn(to_seq(q_lat), to_seq(q_pe),
                           ckv[MP:].reshape(DB, DS, -1), kpe[MP:].reshape(DB, DS, -1),
                           cache_ckv, cache_kpe, li, page_table)
    o_s = jnp.transpose(o_s.reshape(DB, H, DS, -1), (1, 0, 2, 3)).reshape(H, DB * DS, -1)
    o_lat = lax.dynamic_update_slice(o_lat, o_s, (0, MP, 0))
    wuv = jnp.transpose(w_uv, (1, 0, 2)).astype(BF16)
    X = _mla_out(o_lat, wuv, w_o.astype(BF16), X, g, b)
    return X, ckv, kpe


def _np_t5_bucket(rel):
    n = np.maximum(rel, 0)
    exact = N_BUCKETS // 2
    large = exact + (np.log(np.maximum(n, 1) / exact) / math.log(MAX_DISTANCE / exact)
                     * (N_BUCKETS - exact)).astype(np.int64)
    return np.where(n < exact, n, np.minimum(large, N_BUCKETS - 1)).astype(np.int32)


def _bias_tile_kernel(tab_ref, bkt_ref, o_ref):
    bkt = bkt_ref[...]
    for h in range(NSA_HEADS):
        far = tab_ref[N_BUCKETS - 1, h]
        acc = jnp.zeros(bkt.shape, F32)
        for b in range(N_BUCKETS - 1):
            acc = jnp.where(bkt == b, tab_ref[b, h] - far, acc)
        o_ref[h] = acc


def _bias_tiles(table, buckets):
    R, C = buckets.shape
    return pl.pallas_call(
        _bias_tile_kernel,
        in_specs=[pl.BlockSpec(memory_space=pltpu.SMEM), pl.BlockSpec(memory_space=pltpu.VMEM)],
        out_specs=pl.BlockSpec(memory_space=pltpu.VMEM),
        out_shape=jax.ShapeDtypeStruct((NSA_HEADS, R, C), F32),
        name="nsa_bias_tiles",
    )(table, jnp.asarray(buckets))


def _hi_lo(a):
    hi = a.astype(BF16)
    lo = (a - hi.astype(F32)).astype(BF16)
    return hi, lo


def _nsa_in_kernel(x_ref, w_ref, q_ref, kv_ref, kvb_ref, gate_ref):
    h = _dot(x_ref[...].astype(BF16), w_ref[...])
    qw = NSA_HEADS * LANES
    for hd in range(NSA_HEADS):
        q_ref[hd] = (h[:, hd * LANES:(hd + 1) * LANES] * NSA_SCALE).astype(BF16)
    for j in range(6):
        piece = h[:, qw + j * LANES:qw + (j + 1) * LANES]
        kv_ref[j] = piece
        kvb_ref[j] = piece.astype(BF16)
    gate_ref[...] = jax.nn.sigmoid(h[:, qw + 6 * LANES:])


def _nsa_in(x, w_ext, tm=ROW_TILE):
    M, D = x.shape
    tm = math.gcd(M, tm)
    NW = w_ext.shape[1]
    return pl.pallas_call(
        _nsa_in_kernel,
        grid=(M // tm,),
        in_specs=[pl.BlockSpec((tm, D), lambda i: (i, 0)),
                  pl.BlockSpec((D, NW), lambda i: (0, 0))],
        out_specs=[pl.BlockSpec((NSA_HEADS, tm, LANES), lambda i: (0, i, 0)),
                   pl.BlockSpec((6, tm, LANES), lambda i: (0, i, 0)),
                   pl.BlockSpec((6, tm, LANES), lambda i: (0, i, 0)),
                   pl.BlockSpec((tm, LANES), lambda i: (i, 0))],
        out_shape=[jax.ShapeDtypeStruct((NSA_HEADS, M, LANES), BF16),
                   jax.ShapeDtypeStruct((6, M, LANES), F32),
                   jax.ShapeDtypeStruct((6, M, LANES), BF16),
                   jax.ShapeDtypeStruct((M, LANES), F32)],
        compiler_params=_cparams("parallel"),
        name="nsa_in",
    )(x, w_ext)


def _pad_heads_to_group_lanes(w, axis):
    z = jnp.zeros_like(w)
    lo = jnp.concatenate([w, z], axis + 1)
    hi = jnp.concatenate([z, w], axis + 1)
    shape = [1] * w.ndim
    shape[axis] = NSA_HEADS
    first = (jnp.arange(NSA_HEADS) < NSA_HPG).reshape(shape)
    return jnp.where(first, lo, hi)


def _compress_weights(pe, w1, w2):
    eye = jnp.eye(NSA_GROUPS, dtype=F32)
    half = lambda w: jnp.einsum('rde,gh->rgdhe', w, eye).reshape(CMP_STRIDE * LANES, LANES)
    w_big = jnp.concatenate([half(w1[:CMP_STRIDE]), half(w1[CMP_STRIDE:])], 1)
    flat = lambda p: jnp.tile(p[:, None, :], (1, NSA_GROUPS, 1)).reshape(1, CMP_STRIDE * LANES)
    pe2 = jnp.concatenate([flat(pe[:CMP_STRIDE]), flat(pe[CMP_STRIDE:]),
                           jnp.zeros((6, CMP_STRIDE * LANES), F32)], 0)
    return w_big.astype(BF16), pe2.astype(BF16), jnp.kron(eye, w2).astype(BF16)


def _compress_blocks(fs, const, w2):
    nh = fs.shape[0]
    nxt = pltpu.roll(fs[:, LANES:], nh - 1, 0)
    pre = fs[:, :LANES] + nxt + const
    out = _dot((pre * jax.nn.sigmoid(pre)).astype(BF16), w2)
    row = lax.broadcasted_iota(jnp.int32, out.shape, 0)
    return jnp.where(row < nh - 1, out, 0.0).astype(BF16)


def _nsa_compress_kernel(x_ref, w_ref, pe_ref, w2_ref, o_ref, c_ref):
    fs = _dot(x_ref[...], w_ref[...])
    cst = _dot(pe_ref[...], w_ref[...])
    const = cst[0:1, :LANES] + cst[1:2, LANES:]
    c_ref[...] = jnp.broadcast_to(const, c_ref.shape)
    o_ref[...] = _compress_blocks(fs, const, w2_ref[...])


def _nsa_compress(x, w_big, pe2, w2):
    _, B, nh, K = x.shape
    return pl.pallas_call(
        _nsa_compress_kernel,
        grid=(2, B),
        in_specs=[pl.BlockSpec((None, None, nh, K), lambda t, b: (t, b, 0, 0)),
                  pl.BlockSpec((None, K, 2 * LANES), lambda t, b: (t, 0, 0)),
                  pl.BlockSpec((None, 8, K), lambda t, b: (t, 0, 0)),
                  pl.BlockSpec((None, LANES, LANES), lambda t, b: (t, 0, 0))],
        out_specs=[pl.BlockSpec((None, None, nh, LANES), lambda t, b: (t, b, 0, 0)),
                   pl.BlockSpec((None, None, 8, LANES), lambda t, b: (t, b, 0, 0))],
        out_shape=[jax.ShapeDtypeStruct((2, B, nh, LANES), BF16),
                   jax.ShapeDtypeStruct((2, B, 8, LANES), F32)],
        compiler_params=_cparams("parallel", "parallel"),
        name="nsa_compress",
    )(x, w_big, pe2, w2)


def _top_blocks(work, blk, n_pick):
    ns = work.shape[-1]
    sel = jnp.zeros(work.shape, F32)
    for _ in range(n_pick):
        best = jnp.max(work, -1, keepdims=True)
        first = jnp.min(jnp.where(work == best, blk, ns), -1, keepdims=True)
        hit = blk == first
        sel = jnp.where(hit, 1.0, sel)
        work = jnp.where(hit, NEG_INF, work)
    return sel


def _block_importance(p_grp, ns):
    nh = p_grp.shape[-1]
    cc = lax.broadcasted_iota(jnp.int32, (nh, ns), 0)
    jj = lax.broadcasted_iota(jnp.int32, (nh, ns), 1) * (SEL_BLOCK // CMP_STRIDE)
    cover = jnp.where((cc >= jj - 1) & (cc <= jj + 3), 1.0, 0.0).astype(BF16)
    hi = p_grp.astype(BF16)
    r1 = p_grp - hi.astype(F32)
    mid = r1.astype(BF16)
    lo = (r1 - mid.astype(F32)).astype(BF16)
    return _dot(hi, cover) + _dot(mid, cover) + _dot(lo, cover)


def _nsa_prompt_kernel(q_ref, kc_ref, vc_ref, ks_ref, vs_ref, kw_ref, vw_ref, gate_ref, akey_ref, aband_ref,
                       o_ref, m_ref, l_ref, acc_ref, *, tk):
    H, TQ = q_ref.shape[0], q_ref.shape[1]
    R = H * TQ
    G = NSA_GROUPS
    S = ks_ref.shape[0]
    nh = kc_ref.shape[0]
    ns = S // SEL_BLOCK
    qb = pl.program_id(1)
    q0 = qb * TQ
    q = q_ref[...].reshape(R, LANES)
    akey = akey_ref[...].reshape(R, 2 * BAND)
    i32 = jnp.int32

    def key_band(k_first, n_keys):
        d = k_first - q0 + Q_BLOCK + lax.broadcasted_iota(i32, (n_keys, 2 * BAND), 0)
        slot = lax.broadcasted_iota(i32, (n_keys, 2 * BAND), 1) & (BAND - 1)
        return _dot_nt(akey, jnp.where(d == slot, 1.0, 0.0).astype(BF16))

    c_oh = lax.broadcasted_iota(i32, (nh, LANES), 0) - (qb * (Q_BLOCK // CMP_STRIDE) - CMP_BAND_LEAD)
    m_oh = lax.broadcasted_iota(i32, (nh, LANES), 1)
    oh = jnp.where((c_oh == (m_oh & (CMP_BAND - 1))) & (m_oh < 2 * CMP_BAND), 1.0, 0.0).astype(BF16)
    s_c = _dot_nt(q, kc_ref[...]) + _dot_nt(aband_ref[...].reshape(R, LANES), oh)
    c_end = lax.broadcasted_iota(i32, (TQ, nh), 1) * CMP_STRIDE + (CMP_LEN - 1)
    q_pos = q0 + lax.broadcasted_iota(i32, (TQ, nh), 0)
    s_c = s_c.reshape(H, TQ, nh) + jnp.where(c_end <= q_pos, 0.0, NEG_INF)[None]
    m_c = jnp.max(s_c, -1, keepdims=True)
    m_c = jnp.where(m_c == NEG_INF, 0.0, m_c)
    e_c = jnp.exp(s_c - m_c)
    d_c = jnp.sum(e_c, -1, keepdims=True)
    p_c = e_c / jnp.where(d_c > 0, d_c, 1.0)
    o_c = _dot(p_c.reshape(R, nh).astype(BF16), vc_ref[...])

    p_grp = jnp.sum(p_c.reshape(G, NSA_HPG, TQ, nh), axis=1).reshape(G * TQ, nh)
    imp = _block_importance(p_grp, ns)
    blk = lax.broadcasted_iota(i32, (G * TQ, ns), 1)
    cur = (q0 + (lax.broadcasted_iota(i32, (G * TQ, ns), 0) & (TQ - 1))) >> SEL_SHIFT
    forced = (blk == 0) | (blk == cur) | (blk == cur - 1)
    work = jnp.where(forced, jnp.inf, jnp.where(blk <= cur, imp, NEG_INF))
    sel_b = _top_blocks(work, blk, min(N_SEL, ns)).astype(BF16)

    m_ref[...] = jnp.full_like(m_ref, NEG_BIG)
    l_ref[...] = jnp.zeros_like(l_ref)
    acc_ref[...] = jnp.zeros_like(acc_ref)

    def sel_chunk(kb, near):
        k0 = pl.multiple_of(kb * tk, tk)
        s = _dot_nt(q, ks_ref[pl.ds(k0, tk), :])
        eb = lax.broadcasted_iota(i32, (ns, tk), 0)
        ek = (k0 + lax.broadcasted_iota(i32, (ns, tk), 1)) >> SEL_SHIFT
        allow = _dot(sel_b, jnp.where(eb == ek, 1.0, 0.0).astype(BF16))
        mask = (allow - 1.0) * (-NEG_BIG)
        if near:
            s = s + key_band(k0, tk)
            qpos = q0 + (lax.broadcasted_iota(i32, (G * TQ, tk), 0) & (TQ - 1))
            kpos = k0 + lax.broadcasted_iota(i32, (G * TQ, tk), 1)
            mask = jnp.where(kpos <= qpos, mask, NEG_BIG)
        s = (s.reshape(G, NSA_HPG, TQ, tk) + mask.reshape(G, 1, TQ, tk)).reshape(R, tk)
        _online_softmax_step(s, vs_ref[pl.ds(k0, tk), :], m_ref, l_ref, acc_ref)

    n_last = q0 // tk
    n_far = jnp.maximum(q0 - Q_BLOCK, 0) // tk

    def far_body(kb, carry):
        sel_chunk(kb, False)
        return carry

    lax.fori_loop(0, n_far, far_body, 0)

    @pl.when(n_far < n_last)
    def _():
        sel_chunk(n_far, True)

    sel_chunk(n_last, True)
    o_s = acc_ref[...] / l_ref[...]

    nw = WINDOW + TQ
    w0 = pl.multiple_of(jnp.maximum(q0 - WINDOW, 0), TQ)
    s_w = _dot_nt(q, kw_ref[pl.ds(w0, nw), :]) + key_band(w0, nw)
    rel = q0 + lax.broadcasted_iota(i32, (TQ, nw), 0) - (w0 + lax.broadcasted_iota(i32, (TQ, nw), 1))
    s_w = s_w.reshape(H, TQ, nw) + jnp.where((rel >= 0) & (rel < WINDOW), 0.0, NEG_INF)[None]
    e_w = jnp.exp(s_w - jnp.max(s_w, -1, keepdims=True))
    d_w = jnp.sum(e_w, -1, keepdims=True).reshape(R, 1)
    o_w = _dot(e_w.reshape(R, nw).astype(BF16), vw_ref[pl.ds(w0, nw), :]) / d_w

    gates = gate_ref[...]
    for h in range(H):
        rows = slice(h * TQ, (h + 1) * TQ)
        o_h = (gates[:, h:h + 1] * o_c[rows] + gates[:, H + h:H + h + 1] * o_s[rows]
               + gates[:, 2 * H + h:2 * H + h + 1] * o_w[rows])
        o_ref[:, h * LANES:(h + 1) * LANES] = o_h.astype(o_ref.dtype)


def _nsa_prompt_attn(q, cmp_kv, kvb, gates, akey, aband, B, S, m_total):
    H = q.shape[0]
    nq = S // Q_BLOCK
    nh = cmp_kv.shape[2]
    tk = min(SEL_CHUNK, S)
    kv3 = kvb[:, :B * S].reshape(6, B, S, LANES)

    def stream(j):
        return pl.BlockSpec((None, None, S, LANES), lambda b, i: (j, b, 0, 0))

    def cmp(j):
        return pl.BlockSpec((None, None, nh, LANES), lambda b, i: (j, b, 0, 0))

    return pl.pallas_call(
        functools.partial(_nsa_prompt_kernel, tk=tk),
        grid=(B, nq),
        in_specs=[pl.BlockSpec((H, Q_BLOCK, LANES), lambda b, i: (0, b * nq + i, 0)),
                  cmp(0), cmp(1), stream(2), stream(3), stream(4), stream(5),
                  pl.BlockSpec((Q_BLOCK, LANES), lambda b, i: (b * nq + i, 0)),
                  pl.BlockSpec((H, Q_BLOCK, 2 * BAND), lambda b, i: (0, 0, 0)),
                  pl.BlockSpec((H, Q_BLOCK, LANES), lambda b, i: (0, 0, 0))],
        out_specs=pl.BlockSpec((Q_BLOCK, H * LANES), lambda b, i: (b * nq + i, 0)),
        out_shape=jax.ShapeDtypeStruct((m_total, H * LANES), BF16),
        scratch_shapes=[pltpu.VMEM((H * Q_BLOCK, 1), F32), pltpu.VMEM((H * Q_BLOCK, 1), F32),
                        pltpu.VMEM((H * Q_BLOCK, LANES), F32)],
        compiler_params=_cparams("parallel", "parallel"),
        name="nsa_prompt_attn",
    )(q, cmp_kv, cmp_kv, kv3, kv3, kv3, kv3, gates, akey, aband)


def _feature_major_pages(cache):
    L, POOL = cache.shape[:2]
    return jnp.transpose(cache, (0, 1, 3, 4, 2)).reshape(L, POOL, LANES, cache.shape[2])


def _nsa_fs_kernel(pt_ref, w_ref, *rest):
    page_refs, (o_ref, rows_ref) = rest[:-2], rest[-2:]
    for i, r in enumerate(page_refs):
        rows_ref[i * PAGE_SIZE:(i + 1) * PAGE_SIZE, :] = r[...].T
    n_half = o_ref.shape[0]
    acc = None
    for pos in range(CMP_STRIDE):
        x = rows_ref[pl.ds(pos, n_half, stride=CMP_STRIDE), :].astype(BF16)
        part = _dot(x, w_ref[pos])
        acc = part if acc is None else acc + part
    o_ref[...] = acc


def _nsa_fs_sample(cache, li, page_table, w_big, pages=16):
    DB, n_pages = page_table.shape
    pages = min(pages, n_pages)
    hb = PAGE_SIZE // CMP_STRIDE
    view = _feature_major_pages(cache)
    w_pos = w_big.reshape(CMP_STRIDE, LANES, 2 * LANES)

    def page_spec(i):
        return pl.BlockSpec((None, None, LANES, PAGE_SIZE), lambda b, j, pt: (li, pt[b, j * pages + i], 0, 0))

    return pl.pallas_call(
        _nsa_fs_kernel,
        grid_spec=pltpu.PrefetchScalarGridSpec(
            num_scalar_prefetch=1,
            grid=(DB, n_pages // pages),
            in_specs=[pl.BlockSpec((CMP_STRIDE, LANES, 2 * LANES), lambda b, j, pt: (0, 0, 0))]
                     + [page_spec(i) for i in range(pages)],
            out_specs=pl.BlockSpec((None, pages * hb, 2 * LANES), lambda b, j, pt: (b, j, 0)),
            scratch_shapes=[pltpu.VMEM((pages * PAGE_SIZE, LANES), F32)]),
        out_shape=jax.ShapeDtypeStruct((DB, n_pages * hb, 2 * LANES), F32),
        compiler_params=_cparams("parallel", "parallel"),
        name="nsa_fs_sample",
    )(page_table, w_pos, *([view] * pages))


def _nsa_sample_kernel(pt_ref, q_ref, fsk_ref, fsv_ref, const_ref, w2_ref, skw_ref, svw_ref, new_ref, gate_ref,
                       tile_ref, *rest, pages, ds, past):
    ks_refs = rest[:pages]
    vs_refs = rest[pages:2 * pages]
    o_ref, sel_ref, oc_ref, m_ref, l_ref, acc_ref = rest[2 * pages:]
    j = pl.program_id(1)
    n_steps = pl.num_programs(1)
    H, G = NSA_HEADS, NSA_GROUPS
    R = ds * H
    nh = fsk_ref.shape[0]
    ns = past // SEL_BLOCK
    win = skw_ref.shape[1]
    i32 = jnp.int32
    q = q_ref[...]
    tok = lax.broadcasted_iota(i32, (R, 1), 0) >> HEAD_SHIFT

    @pl.when(j == 0)
    def _():
        kcb = _compress_blocks(fsk_ref[...], const_ref[0], w2_ref[0])
        vcb = _compress_blocks(fsv_ref[...], const_ref[1], w2_ref[1])
        s_c = _dot_nt(q, kcb) + tile_ref[:, TILE_CMP:TILE_CMP + nh]
        c_end = lax.broadcasted_iota(i32, (R, nh), 1) * CMP_STRIDE + (CMP_LEN - 1)
        s_c = jnp.where(c_end <= past + tok, s_c, NEG_INF)
        m_c = jnp.max(s_c, -1, keepdims=True)
        m_c = jnp.where(m_c == NEG_INF, 0.0, m_c)
        e_c = jnp.exp(s_c - m_c)
        d_c = jnp.sum(e_c, -1, keepdims=True)
        p_c = e_c / jnp.where(d_c > 0, d_c, 1.0)
        oc_ref[...] = _dot(p_c.astype(BF16), vcb)
        p_grp = jnp.sum(p_c.reshape(ds * G, NSA_HPG, nh), axis=1)
        imp = _block_importance(p_grp, ns)
        blk = lax.broadcasted_iota(i32, (ds * G, ns), 1)
        work = jnp.where((blk == 0) | (blk == ns - 1), jnp.inf, imp)
        sel_ref[...] = _top_blocks(work, blk, min(N_SEL - 1, ns))
        m_ref[...] = jnp.full_like(m_ref, NEG_BIG)
        l_ref[...] = jnp.zeros_like(l_ref)
        acc_ref[...] = jnp.zeros_like(acc_ref)

    def online(s, v_list, feature_major):
        pv_dot = _dot_nt if feature_major else _dot
        m_prev = m_ref[...]
        m_new = jnp.maximum(m_prev, jnp.max(s, -1, keepdims=True))
        alpha = jnp.exp(m_prev - m_new)
        p = jnp.exp(s - m_new)
        l_ref[...] = alpha * l_ref[...] + jnp.sum(p, -1, keepdims=True)
        pb = p.astype(BF16)
        width = s.shape[1] // len(v_list)
        pv = pv_dot(pb[:, :width], v_list[0])
        for i in range(1, len(v_list)):
            pv += pv_dot(pb[:, i * width:(i + 1) * width], v_list[i])
        acc_ref[...] = alpha * acc_ref[...] + pv
        m_ref[...] = m_new

    nk = pages * PAGE_SIZE
    is_last = (j == n_steps - 1).astype(F32)
    s_parts = []
    for i in range(pages):
        s_i = _dot(q, ks_refs[i][...].astype(BF16))
        if i == pages - 1:
            s_i = s_i + is_last * tile_ref[:, TILE_LAST:TILE_LAST + PAGE_SIZE]
        s_parts.append(s_i)
    s = jnp.concatenate(s_parts, axis=1)
    eb = lax.broadcasted_iota(i32, (ns, nk), 0)
    ek = (j * nk + lax.broadcasted_iota(i32, (ns, nk), 1)) >> SEL_SHIFT
    allow = _dot(sel_ref[...].astype(BF16), jnp.where(eb == ek, 1.0, 0.0).astype(BF16))
    mask = (allow - 1.0) * (-NEG_BIG)
    s = (s.reshape(ds * G, NSA_HPG, nk) + mask[:, None, :]).reshape(R, nk)
    online(s, [r[...].astype(BF16) for r in vs_refs], True)

    @pl.when(j == n_steps - 1)
    def _():
        new_b = new_ref[...].astype(BF16)
        u = lax.broadcasted_iota(i32, (R, ds), 1)
        nb = tile_ref[:, TILE_NEW:TILE_NEW + ds] + jnp.where(u <= tok, 0.0, NEG_INF)
        online(_dot_nt(q, new_b[0]) + nb, [new_b[1]], False)
        o_s = acc_ref[...] / l_ref[...]
        idx = lax.broadcasted_iota(i32, (R, win), 1)
        s1 = _dot(q, skw_ref[...].astype(BF16)) + tile_ref[:, TILE_WIN:TILE_WIN + win]
        s1 = jnp.where(idx > tok + (win - WINDOW), s1, NEG_INF)
        s2 = _dot_nt(q, new_b[2]) + nb
        m_w = jnp.maximum(jnp.max(s1, -1, keepdims=True), jnp.max(s2, -1, keepdims=True))
        e1 = jnp.exp(s1 - m_w)
        e2 = jnp.exp(s2 - m_w)
        d_w = jnp.sum(e1, -1, keepdims=True) + jnp.sum(e2, -1, keepdims=True)
        o_w = (_dot_nt(e1.astype(BF16), svw_ref[...].astype(BF16)) + _dot(e2.astype(BF16), new_b[3])) / d_w
        g = gate_ref[...]
        o_ref[...] = (g[0] * oc_ref[...] + g[1] * o_s + g[2] * o_w).astype(o_ref.dtype)


def _nsa_sample_attn(q_s, fs_k, fs_v, const, w2, state_kw, state_vw, new4, gates_s, tiles,
                     cache_ks, cache_vs, li, page_table, pages=8):
    DB, R, _ = q_s.shape
    ds = new4.shape[2]
    n_pages = page_table.shape[1]
    pages = min(pages, n_pages)
    past = n_pages * PAGE_SIZE
    nh = fs_k.shape[1]
    win = state_kw.shape[2]
    L, POOL = cache_ks.shape[:2]
    ks_view = _feature_major_pages(cache_ks)
    vs_view = _feature_major_pages(cache_vs)
    skw = _feature_major_pages(state_kw)
    svw = _feature_major_pages(state_vw)

    def page_spec(i):
        return pl.BlockSpec((None, None, LANES, PAGE_SIZE), lambda b, j, pt: (li, pt[b, j * pages + i], 0, 0))

    per_seq = lambda *shape: pl.BlockSpec((None,) + shape, lambda b, j, pt: (b,) + (0,) * len(shape))
    in_specs = [per_seq(R, LANES), per_seq(nh, 2 * LANES), per_seq(nh, 2 * LANES),
                pl.BlockSpec((2, 1, LANES), lambda b, j, pt: (0, 0, 0)),
                pl.BlockSpec((2, LANES, LANES), lambda b, j, pt: (0, 0, 0)),
                pl.BlockSpec((None, None, LANES, win), lambda b, j, pt: (li, b, 0, 0)),
                pl.BlockSpec((None, None, LANES, win), lambda b, j, pt: (li, b, 0, 0)),
                per_seq(4, ds, LANES), per_seq(3, R, 1),
                pl.BlockSpec(tiles.shape, lambda b, j, pt: (0, 0))]
    in_specs += [page_spec(i) for i in range(pages)] * 2
    return pl.pallas_call(
        functools.partial(_nsa_sample_kernel, pages=pages, ds=ds, past=past),
        grid_spec=pltpu.PrefetchScalarGridSpec(
            num_scalar_prefetch=1,
            grid=(DB, n_pages // pages),
            in_specs=in_specs,
            out_specs=per_seq(R, LANES),
            scratch_shapes=[pltpu.VMEM((ds * NSA_GROUPS, past // SEL_BLOCK), F32),
                            pltpu.VMEM((R, LANES), F32),
                            pltpu.VMEM((R, 1), F32), pltpu.VMEM((R, 1), F32), pltpu.VMEM((R, LANES), F32)]),
        out_shape=jax.ShapeDtypeStruct((DB, R, LANES), BF16),
        compiler_params=_cparams("parallel", "arbitrary"),
        name="nsa_sample_attn",
    )(page_table, q_s, fs_k, fs_v, const, w2, skw, svw, new4, gates_s, tiles,
      *([ks_view] * pages), *([vs_view] * pages))


def _nsa_bias_inputs(table, ds, past, win):
    i = np.arange(Q_BLOCK)[:, None]
    rel_key = i - np.arange(BAND)[None, :] + Q_BLOCK
    rel_cmp = i - CMP_STRIDE * (np.arange(CMP_BAND)[None, :] - CMP_BAND_LEAD) - (CMP_LEN - 1)
    sheet_p = np.concatenate([rel_key, rel_cmp, np.zeros((Q_BLOCK, LANES - CMP_BAND), np.int64)], 1)
    tp = _bias_tiles(table, _np_t5_bucket(sheet_p))
    hi, lo = _hi_lo(tp[:, :, :BAND])
    akey = jnp.concatenate([hi, lo], -1)
    hi, lo = _hi_lo(tp[:, :, BAND:BAND + CMP_BAND])
    aband = jnp.concatenate([hi, lo, jnp.zeros((NSA_HEADS, Q_BLOCK, LANES - 2 * CMP_BAND), BF16)], -1)
    t = np.arange(ds)[:, None]
    nh = past // CMP_STRIDE
    assert nh <= TILE_WIN - TILE_CMP and win <= TILE_LAST - TILE_WIN and ds <= SEL_BLOCK
    pad = lambda a, w: np.concatenate([a, np.zeros((ds, w - a.shape[1]), np.int64)], 1)
    rel_c = past + t - CMP_STRIDE * np.arange(nh)[None, :] - (CMP_LEN - 1)
    rel_w = t + win - np.arange(win)[None, :]
    rel_l = t + PAGE_SIZE - np.arange(PAGE_SIZE)[None, :]
    rel_n = t - np.arange(ds)[None, :]
    sheet_s = np.concatenate([pad(rel_c, TILE_WIN - TILE_CMP), pad(rel_w, TILE_LAST - TILE_WIN),
                              pad(rel_l, TILE_NEW - TILE_LAST), pad(rel_n, LANES)], 1)
    ts = _bias_tiles(table, _np_t5_bucket(sheet_s))
    tiles = jnp.transpose(ts, (1, 0, 2)).reshape(ds * NSA_HEADS, -1)
    return akey, aband, tiles


def _nsa_layer(X, dims, li, bias_inputs, cache_kc, cache_vc, cache_ks, cache_vs, state_kw, state_vw, page_table,
               w_in, pe_k, w1_k, w2_k, pe_v, w1_v, w2_v, w_o, g, b):
    B, S, DB, DS = dims
    MP = B * S
    M, D = X.shape
    H = NSA_HEADS
    akey, aband, tiles = bias_inputs
    qw = H * NSA_DH
    kvw = 6 * NSA_GROUPS * NSA_DH
    wq = _pad_heads_to_group_lanes(w_in[:, :qw].reshape(D, H, NSA_DH), 1).reshape(D, H * LANES)
    w_ext = jnp.concatenate([wq, w_in[:, qw:], jnp.zeros((D, LANES - (w_in.shape[1] - qw - kvw)), F32)], 1)
    q, kv, kvb, gates = _nsa_in(X, w_ext.astype(BF16))
    wk_big, pe2_k, w2k = _compress_weights(pe_k, w1_k, w2_k)
    wv_big, pe2_v, w2v = _compress_weights(pe_v, w1_v, w2_v)
    w2 = jnp.stack([w2k, w2v])
    nh = S // CMP_STRIDE
    cmp_kv, const = _nsa_compress(kvb[:2, :MP].reshape(2, B, nh, CMP_STRIDE * LANES),
                                  jnp.stack([wk_big, wv_big]), jnp.stack([pe2_k, pe2_v]), w2)
    o = _nsa_prompt_attn(q, cmp_kv, kvb, gates, akey, aband, B, S, M)
    q_s = jnp.transpose(q[:, MP:].reshape(H, DB, DS, LANES), (1, 2, 0, 3)).reshape(DB, DS * H, LANES)
    fs_k = _nsa_fs_sample(cache_kc, li, page_table, wk_big)
    fs_v = _nsa_fs_sample(cache_vc, li, page_table, wv_big)
    new4 = jnp.transpose(kv[2:, MP:].reshape(4, DB, DS, LANES), (1, 0, 2, 3))
    gates_s = jnp.transpose(gates[MP:, :3 * H].reshape(DB, DS, 3, H), (0, 2, 1, 3)).reshape(DB, 3, DS * H, 1)
    o_s = _nsa_sample_attn(q_s, fs_k, fs_v, const[:, 0, :1], w2, state_kw, state_vw, new4, gates_s, tiles,
                           cache_ks, cache_vs, li, page_table)
    o = lax.dynamic_update_slice(o, o_s.reshape(DB * DS, H * LANES), (MP, 0))
    wo = _pad_heads_to_group_lanes(w_o.reshape(H, NSA_DH, D), 0).reshape(H * LANES, D)
    X = _mm_ln(o, wo.astype(BF16), X, g, b, name="nsa_out")
    split = lambda a: (a[:MP].reshape(B, S, NSA_GROUPS, NSA_DH), a[MP:].reshape(DB, DS, NSA_GROUPS, NSA_DH))
    pieces = [split(kv[j]) for j in range(6)]
    keep = min(WINDOW, S)
    outs_p = [p for p, _ in pieces[:4]] + [pieces[4][0][:, S - keep:], pieces[5][0][:, S - keep:]]
    outs_s = [s for _, s in pieces[:4]] + [jnp.concatenate([state_kw[li], pieces[4][1]], 1)[:, DS:],
                                           jnp.concatenate([state_vw[li], pieces[5][1]], 1)[:, DS:]]
    return X, outs_p, outs_s


def kernel(x_prompt, x_sample, cache_mla_ckv, cache_mla_kpe, cache_nsa_kc, cache_nsa_vc, cache_nsa_ks, cache_nsa_vs, state_nsa_kw, state_nsa_vw, cache_mem_k, cache_mem_v, page_table, mem_prompt, ln_g, ln_b, mla_w_down, mla_q_norm, mla_kv_norm, mla_w_uq, mla_w_uk, mla_w_uv, mla_w_o, nsa_w_in, nsa_cmp_pe_k, nsa_cmp_w1_k, nsa_cmp_w2_k, nsa_cmp_pe_v, nsa_cmp_w1_v, nsa_cmp_w2_v, nsa_w_o, rel_bias, xa_w_q, xa_w_kv, xa_w_o, ffn_w_gu, ffn_w_down, moe_router, moe_w_gu, moe_w_down):
    B, S, D = x_prompt.shape
    DB, DS, _ = x_sample.shape
    dims = (B, S, DB, DS)
    MP, MS = B * S, DB * DS
    past = page_table.shape[1] * PAGE_SIZE
    n_mem = mem_prompt.shape[1]
    depth = ln_g.shape[0]
    HD = MEM_HEADS * MEM_DH

    X = jnp.concatenate([x_prompt.reshape(MP, D), x_sample.reshape(MS, D)], 0)
    pos = jnp.concatenate([jnp.tile(jnp.arange(S, dtype=jnp.int32), B),
                           jnp.tile(past + jnp.arange(DS, dtype=jnp.int32), DB)])
    cos, sin = _rope_tables(pos)
    mem_rows = mem_prompt.reshape(B * n_mem, D)
    bias_inputs = _nsa_bias_inputs(rel_bias, DS, past, state_nsa_kw.shape[2])

    outs_p = {k: [] for k in ("ckv", "kpe", "kc", "vc", "ks", "vs", "kw", "vw", "mk", "mv")}
    outs_s = {k: [] for k in ("ckv", "kpe", "kc", "vc", "ks", "vs", "kw", "vw")}
    for i in range(depth):
        li = i // 2
        if i % 2 == 0:
            X, ckv, kpe = _mla_layer(X, dims, cos, sin, cache_mla_ckv, cache_mla_kpe, li, page_table,
                                     mla_w_down[li], mla_q_norm[li], mla_kv_norm[li], mla_w_uq[li],
                                     mla_w_uk[li], mla_w_uv[li], mla_w_o[li], ln_g[i, 0], ln_b[i, 0])
            outs_p["ckv"].append(ckv[:MP].reshape(B, S, -1))
            outs_p["kpe"].append(kpe[:MP].reshape(B, S, -1))
            outs_s["ckv"].append(ckv[MP:].reshape(DB, DS, -1))
            outs_s["kpe"].append(kpe[MP:].reshape(DB, DS, -1))
        else:
            X, o_p, o_s = _nsa_layer(X, dims, li, bias_inputs, cache_nsa_kc, cache_nsa_vc, cache_nsa_ks,
                                     cache_nsa_vs, state_nsa_kw, state_nsa_vw, page_table,
                                     nsa_w_in[li], nsa_cmp_pe_k[li], nsa_cmp_w1_k[li], nsa_cmp_w2_k[li],
                                     nsa_cmp_pe_v[li], nsa_cmp_w1_v[li], nsa_cmp_w2_v[li], nsa_w_o[li],
                                     ln_g[i, 0], ln_b[i, 0])
            for k, v in zip(("kc", "vc", "ks", "vs", "kw", "vw"), o_p):
                outs_p[k].append(v)
            for k, v in zip(("kc", "vc", "ks", "vs", "kw", "vw"), o_s):
                outs_s[k].append(v)
        kv = _mm(mem_rows, xa_w_kv[i].astype(BF16), name="mem_kv").reshape(B, n_mem, 2, MEM_HEADS, MEM_DH)
        mk, mv = kv[:, :, 0], kv[:, :, 1]
        outs_p["mk"].append(mk)
        outs_p["mv"].append(mv)
        q = _mm(X, xa_w_q[i].astype(BF16), out_dtype=BF16, scale=MEM_SCALE, name="xattn_q")
        o_p = _xattn(q[:MP].reshape(B, S, HD), mk.reshape(B, n_mem, HD), mv.reshape(B, n_mem, HD),
                     rows=min(ROW_TILE, S), group=1)
        o_s = _xattn(q[MP:].reshape(DB, DS, HD), cache_mem_k[i].reshape(DB, n_mem, HD),
                     cache_mem_v[i].reshape(DB, n_mem, HD), rows=DS, group=min(8, DB))
        o = jnp.concatenate([o_p.reshape(MP, HD), o_s.reshape(MS, HD)], 0)
        X = _mm_ln(o, xa_w_o[i].astype(BF16), X, ln_g[i, 1], ln_b[i, 1], name="xattn_out")
        if i % 2 == 0:
            X = _ffn_ln(X, ffn_w_gu[li][None].astype(BF16), ffn_w_down[li][None].astype(BF16),
                        ln_g[i, 2], ln_b[i, 2])
        else:
            comb = _router(X, moe_router[li])
            X = _ffn_ln(X, moe_w_gu[li].astype(BF16), moe_w_down[li].astype(BF16),
                        ln_g[i, 2], ln_b[i, 2], comb=comb)

    st = jnp.stack
    return (X[:MP].reshape(B, S, D), X[MP:].reshape(DB, DS, D),
            st(outs_p["ckv"]), st(outs_p["kpe"]), st(outs_p["kc"]), st(outs_p["vc"]),
            st(outs_p["ks"]), st(outs_p["vs"]), st(outs_p["kw"]), st(outs_p["vw"]),
            st(outs_p["mk"]), st(outs_p["mv"]),
            st(outs_s["ckv"]), st(outs_s["kpe"]), st(outs_s["kc"]), st(outs_s["vc"]),
            st(outs_s["ks"]), st(outs_s["vs"]), st(outs_s["kw"]), st(outs_s["vw"]))
```

```python
import functools
import math

import jax
import jax.numpy as jnp
import numpy as np
from jax import lax
from jax.experimental import pallas as pl
from jax.experimental.pallas import tpu as pltpu

F32 = jnp.float32
BF16 = jnp.bfloat16

DEPTH = 4
PAGE_SIZE = 128
MLA_HEADS = 16
MLA_Q_RANK = 384
MLA_KV_RANK = 256
MLA_NOPE = 128
MLA_ROPE = 64
MLA_VDIM = 128
MLA_SCALE = (MLA_NOPE + MLA_ROPE) ** -0.5
ROPE_THETA = 10000.0
NSA_HEADS = 16
NSA_GROUPS = 2
NSA_HPG = NSA_HEADS // NSA_GROUPS
NSA_DH = 64
NSA_SCALE = NSA_DH ** -0.5
CMP_LEN = 32
CMP_STRIDE = 16
SEL_BLOCK = 64
N_SEL = 16
WINDOW = 512
N_BUCKETS = 32
MAX_DISTANCE = 128
MEM_HEADS = 4
MEM_DH = 128
MEM_SCALE = MEM_DH ** -0.5
N_EXPERTS = 8
ALPHA = (2 * DEPTH) ** 0.25
LN_EPS = 1e-5
RMS_EPS = 1e-6
Q_BLOCK = 128

LANES = 128
VMEM_LIMIT = 56 * 1024 * 1024
ROW_TILE = 512
NEG_INF = float("-inf")
NEG_BIG = -1e30

ATTN_HEAD_PARTS = 8
MLA_CHUNK = 1024

SEL_CHUNK = 1024
DECODE_PAGES = 16
BAND = 2 * Q_BLOCK
CMP_BAND = 16
CMP_BAND_LEAD = 9
SEL_SHIFT = SEL_BLOCK.bit_length() - 1
HEAD_SHIFT = NSA_HEADS.bit_length() - 1
TILE_CMP, TILE_WIN, TILE_LAST, TILE_NEW = 0, 4 * LANES, 8 * LANES, 9 * LANES


def _cparams(*sem):
    return pltpu.CompilerParams(dimension_semantics=tuple(sem), vmem_limit_bytes=VMEM_LIMIT)


def _dot(a, b):
    return jnp.dot(a, b, preferred_element_type=F32)


def _dot_nt(a, b):
    return lax.dot_general(a, b, (((1,), (1,)), ((), ())), preferred_element_type=F32)


def _layer_norm(h, g, b):
    mu = jnp.mean(h, -1, keepdims=True)
    d = h - mu
    var = jnp.mean(d * d, -1, keepdims=True)
    return d * lax.rsqrt(var + LN_EPS) * g + b


def _rms_norm(x, g):
    return x * lax.rsqrt(jnp.mean(x * x, -1, keepdims=True) + RMS_EPS) * g


def _mm_kernel(x_ref, w_ref, o_ref, *, scale):
    acc = _dot(x_ref[...].astype(BF16), w_ref[...])
    if scale is not None:
        acc = acc * scale
    o_ref[...] = acc.astype(o_ref.dtype)


def _mm(x, w, out_dtype=F32, scale=None, tm=ROW_TILE, name="mm"):
    M, K = x.shape
    N = w.shape[1]
    tm = math.gcd(M, tm)
    return pl.pallas_call(
        functools.partial(_mm_kernel, scale=scale),
        grid=(M // tm,),
        in_specs=[pl.BlockSpec((tm, K), lambda i: (i, 0)),
                  pl.BlockSpec((K, N), lambda i: (0, 0))],
        out_specs=pl.BlockSpec((tm, N), lambda i: (i, 0)),
        out_shape=jax.ShapeDtypeStruct((M, N), out_dtype),
        compiler_params=_cparams("parallel"),
        name=name,
    )(x, w)


def _mm_ln_kernel(a_ref, w_ref, res_ref, g_ref, b_ref, o_ref):
    acc = _dot(a_ref[...].astype(BF16), w_ref[...])
    o_ref[...] = _layer_norm(ALPHA * res_ref[...] + acc, g_ref[...], b_ref[...])


def _mm_ln(a, w, res, g, b, tm=ROW_TILE, name="mm_ln"):
    M, K = a.shape
    D = w.shape[1]
    tm = math.gcd(M, tm)
    return pl.pallas_call(
        _mm_ln_kernel,
        grid=(M // tm,),
        in_specs=[pl.BlockSpec((tm, K), lambda i: (i, 0)),
                  pl.BlockSpec((K, D), lambda i: (0, 0)),
                  pl.BlockSpec((tm, D), lambda i: (i, 0)),
                  pl.BlockSpec((1, D), lambda i: (0, 0)),
                  pl.BlockSpec((1, D), lambda i: (0, 0))],
        out_specs=pl.BlockSpec((tm, D), lambda i: (i, 0)),
        out_shape=jax.ShapeDtypeStruct((M, D), F32),
        compiler_params=_cparams("parallel"),
        name=name,
    )(a, w, res, g.reshape(1, D), b.reshape(1, D))


def _router_kernel(x_ref, r_ref, o_ref):
    logits = jnp.dot(x_ref[...], r_ref[...], preferred_element_type=F32,
                     precision=lax.Precision.HIGHEST)
    lane = lax.broadcasted_iota(jnp.int32, logits.shape, 1)
    logits = jnp.where(lane < N_EXPERTS, logits, NEG_INF)
    v1 = jnp.max(logits, -1, keepdims=True)
    i1 = jnp.min(jnp.where(logits == v1, lane, LANES), -1, keepdims=True)
    rest = jnp.where(lane == i1, NEG_INF, logits)
    v2 = jnp.max(rest, -1, keepdims=True)
    i2 = jnp.min(jnp.where(rest == v2, lane, LANES), -1, keepdims=True)
    e2 = jnp.exp(v2 - v1)
    w1 = 1.0 / (1.0 + e2)
    w2 = e2 / (1.0 + e2)
    o_ref[...] = jnp.where(lane == i1, w1, 0.0) + jnp.where(lane == i2, w2, 0.0)


def _router(x, router, tm=ROW_TILE):
    M, D = x.shape
    tm = math.gcd(M, tm)
    r = jnp.pad(router, ((0, 0), (0, LANES - router.shape[1])))
    return pl.pallas_call(
        _router_kernel,
        grid=(M // tm,),
        in_specs=[pl.BlockSpec((tm, D), lambda i: (i, 0)),
                  pl.BlockSpec((D, LANES), lambda i: (0, 0))],
        out_specs=pl.BlockSpec((tm, LANES), lambda i: (i, 0)),
        out_shape=jax.ShapeDtypeStruct((M, LANES), F32),
        compiler_params=_cparams("parallel"),
        name="moe_router",
    )(x, r)


def _ffn_kernel(*refs, n_e, n_f, routed):
    if routed:
        x_ref, comb_ref, wg_ref, wu_ref, wd_ref, g_ref, b_ref, o_ref, acc_ref, xb_ref = refs
    else:
        x_ref, wg_ref, wu_ref, wd_ref, g_ref, b_ref, o_ref, acc_ref, xb_ref = refs
    e = pl.program_id(1)
    f = pl.program_id(2)

    @pl.when((e == 0) & (f == 0))
    def _():
        acc_ref[...] = jnp.zeros_like(acc_ref)
        xb_ref[...] = x_ref[...].astype(BF16)

    xb = xb_ref[...]
    gate = _dot(xb, wg_ref[...])
    up = _dot(xb, wu_ref[...])
    act = gate * jax.nn.sigmoid(gate) * up
    if routed:
        comb = comb_ref[...]
        lane = lax.broadcasted_iota(jnp.int32, comb.shape, 1)
        act = act * jnp.sum(jnp.where(lane == e, comb, 0.0), -1, keepdims=True)
    acc_ref[...] += _dot(act.astype(BF16), wd_ref[...])

    @pl.when((e == n_e - 1) & (f == n_f - 1))
    def _():
        o_ref[...] = _layer_norm(ALPHA * x_ref[...] + acc_ref[...], g_ref[...], b_ref[...])


def _ffn_ln(x, w_gu, w_down, g, b, comb=None, tm=ROW_TILE, tf=1408):
    M, D = x.shape
    tm = math.gcd(M, tm)
    E, F = w_down.shape[0], w_down.shape[1]
    n_f = F // tf
    routed = comb is not None
    in_specs = [pl.BlockSpec((tm, D), lambda i, e, f: (i, 0))]
    args = [x]
    if routed:
        in_specs.append(pl.BlockSpec((tm, LANES), lambda i, e, f: (i, 0)))
        args.append(comb)
    in_specs += [pl.BlockSpec((None, D, tf), lambda i, e, f: (e, 0, f)),
                 pl.BlockSpec((None, D, tf), lambda i, e, f: (e, 0, f + n_f)),
                 pl.BlockSpec((None, tf, D), lambda i, e, f: (e, f, 0)),
                 pl.BlockSpec((1, D), lambda i, e, f: (0, 0)),
                 pl.BlockSpec((1, D), lambda i, e, f: (0, 0))]
    args += [w_gu, w_gu, w_down, g.reshape(1, D), b.reshape(1, D)]
    return pl.pallas_call(
        functools.partial(_ffn_kernel, n_e=E, n_f=n_f, routed=routed),
        grid=(M // tm, E, n_f),
        in_specs=in_specs,
        out_specs=pl.BlockSpec((tm, D), lambda i, e, f: (i, 0)),
        out_shape=jax.ShapeDtypeStruct((M, D), F32),
        scratch_shapes=[pltpu.VMEM((tm, D), F32), pltpu.VMEM((tm, D), BF16)],
        compiler_params=_cparams("parallel", "arbitrary", "arbitrary"),
        name="ffn_ln",
    )(*args)


def _xattn_kernel(q_ref, k_ref, v_ref, o_ref):
    for gi in range(q_ref.shape[0]):
        for h in range(MEM_HEADS):
            sl = slice(h * MEM_DH, (h + 1) * MEM_DH)
            qh = q_ref[gi, :, sl]
            kh = k_ref[gi, :, sl].astype(BF16)
            vh = v_ref[gi, :, sl].astype(BF16)
            s = _dot_nt(qh, kh)
            m = jnp.max(s, -1, keepdims=True)
            p = jnp.exp(s - m)
            p = p / jnp.sum(p, -1, keepdims=True)
            o_ref[gi, :, sl] = _dot(p.astype(BF16), vh).astype(o_ref.dtype)


def _xattn(q, k, v, rows, group):
    NB, R, HD = q.shape
    NM = k.shape[1]
    return pl.pallas_call(
        _xattn_kernel,
        grid=(NB // group, R // rows),
        in_specs=[pl.BlockSpec((group, rows, HD), lambda b, i: (b, i, 0)),
                  pl.BlockSpec((group, NM, HD), lambda b, i: (b, 0, 0)),
                  pl.BlockSpec((group, NM, HD), lambda b, i: (b, 0, 0))],
        out_specs=pl.BlockSpec((group, rows, HD), lambda b, i: (b, i, 0)),
        out_shape=jax.ShapeDtypeStruct((NB, R, HD), BF16),
        compiler_params=_cparams("parallel", "parallel"),
        name="xattn",
    )(q, k, v)


def _rope_tables(pos):
    half = MLA_ROPE // 2
    inv = ROPE_THETA ** (-jnp.arange(half, dtype=F32) / half)
    ang = pos.astype(F32)[:, None] * inv[None, :]
    c, s = jnp.cos(ang), jnp.sin(ang)
    return jnp.concatenate([c, c], -1), jnp.concatenate([-s, s], -1)


def _swap_halves(w):
    half = w.shape[-1] // 2
    return jnp.concatenate([w[..., half:], w[..., :half]], -1)


def _mla_down_kernel(x_ref, w_ref, qn_ref, kvn_ref, cos_ref, sin_ref,
                     cq_ref, ckv_ref, kpe_ref, ckvb_ref, kpeb_ref):
    down = _dot(x_ref[...].astype(BF16), w_ref[...])
    cq_ref[...] = _rms_norm(down[:, :MLA_Q_RANK], qn_ref[...]).astype(BF16)
    ckv = _rms_norm(down[:, MLA_Q_RANK:MLA_Q_RANK + MLA_KV_RANK], kvn_ref[...])
    ckv_ref[...] = ckv
    ckvb_ref[...] = ckv.astype(BF16)
    ra = down[:, 640:640 + MLA_ROPE]
    rb = down[:, 768:768 + MLA_ROPE]
    kpe = ra * cos_ref[...] + rb * sin_ref[...]
    kpe_ref[...] = kpe
    kpeb_ref[...] = kpe.astype(BF16)


def _mla_down(x, w_ext, q_norm, kv_norm, cos, sin, tm=ROW_TILE):
    M, D = x.shape
    tm = math.gcd(M, tm)
    NW = w_ext.shape[1]
    row = lambda n: pl.BlockSpec((tm, n), lambda i: (i, 0))
    full = lambda a, b: pl.BlockSpec((a, b), lambda i: (0, 0))
    return pl.pallas_call(
        _mla_down_kernel,
        grid=(M // tm,),
        in_specs=[row(D), full(D, NW), full(1, MLA_Q_RANK), full(1, MLA_KV_RANK),
                  row(MLA_ROPE), row(MLA_ROPE)],
        out_specs=[row(MLA_Q_RANK), row(MLA_KV_RANK), row(MLA_ROPE), row(MLA_KV_RANK), row(MLA_ROPE)],
        out_shape=[jax.ShapeDtypeStruct((M, MLA_Q_RANK), BF16),
                   jax.ShapeDtypeStruct((M, MLA_KV_RANK), F32),
                   jax.ShapeDtypeStruct((M, MLA_ROPE), F32),
                   jax.ShapeDtypeStruct((M, MLA_KV_RANK), BF16),
                   jax.ShapeDtypeStruct((M, MLA_ROPE), BF16)],
        compiler_params=_cparams("parallel"),
        name="mla_down",
    )(x, w_ext, q_norm.reshape(1, -1), kv_norm.reshape(1, -1), cos, sin)


def _mla_q_kernel(cq_ref, wq_ref, wk_ref, cos_ref, sin_ref, ql_ref, qp_ref):
    cq = cq_ref[...]
    for h in range(wq_ref.shape[0]):
        q = _dot(cq, wq_ref[h])
        nope = q[:, :MLA_NOPE].astype(BF16)
        ra = q[:, MLA_NOPE:MLA_NOPE + MLA_ROPE]
        rb = q[:, MLA_NOPE + LANES:MLA_NOPE + LANES + MLA_ROPE]
        qp_ref[h] = ((ra * cos_ref[...] + rb * sin_ref[...]) * MLA_SCALE).astype(BF16)
        ql_ref[h] = (_dot_nt(nope, wk_ref[h]) * MLA_SCALE).astype(BF16)


def _mla_q(cq, wq_ext, wk, cos, sin, tm=ROW_TILE):
    M = cq.shape[0]
    tm = math.gcd(M, tm)
    H = wq_ext.shape[0]
    hq = math.gcd(H, 4)
    return pl.pallas_call(
        _mla_q_kernel,
        grid=(M // tm, H // hq),
        in_specs=[pl.BlockSpec((tm, MLA_Q_RANK), lambda i, h: (i, 0)),
                  pl.BlockSpec((hq, MLA_Q_RANK, MLA_NOPE + 2 * LANES), lambda i, h: (h, 0, 0)),
                  pl.BlockSpec((hq, MLA_KV_RANK, MLA_NOPE), lambda i, h: (h, 0, 0)),
                  pl.BlockSpec((tm, MLA_ROPE), lambda i, h: (i, 0)),
                  pl.BlockSpec((tm, MLA_ROPE), lambda i, h: (i, 0))],
        out_specs=[pl.BlockSpec((hq, tm, MLA_KV_RANK), lambda i, h: (h, i, 0)),
                   pl.BlockSpec((hq, tm, MLA_ROPE), lambda i, h: (h, i, 0))],
        out_shape=[jax.ShapeDtypeStruct((H, M, MLA_KV_RANK), BF16),
                   jax.ShapeDtypeStruct((H, M, MLA_ROPE), BF16)],
        compiler_params=_cparams("parallel", "parallel"),
        name="mla_q",
    )(cq, wq_ext, wk, cos, sin)


def _online_softmax_step(s, v, m_ref, l_ref, acc_ref):
    m_prev = m_ref[...]
    m_new = jnp.maximum(m_prev, jnp.max(s, -1, keepdims=True))
    alpha = jnp.exp(m_prev - m_new)
    p = jnp.exp(s - m_new)
    l_ref[...] = alpha * l_ref[...] + jnp.sum(p, -1, keepdims=True)
    acc_ref[...] = alpha * acc_ref[...] + _dot(p.astype(BF16), v)
    m_ref[...] = m_new


def _mla_prompt_attn_kernel(ql_ref, qp_ref, ckv_ref, kpe_ref, o_ref, m_ref, l_ref, acc_ref, *, tk):
    H, TQ = ql_ref.shape[0], ql_ref.shape[1]
    q0 = pl.program_id(1) * TQ
    m_ref[...] = jnp.full_like(m_ref, NEG_INF)
    l_ref[...] = jnp.zeros_like(l_ref)
    acc_ref[...] = jnp.zeros_like(acc_ref)
    n_full = q0 // tk
    hp = H // ATTN_HEAD_PARTS
    rp = hp * TQ

    def chunk(kb, masked):
        k0 = pl.multiple_of(kb * tk, tk)
        kc = ckv_ref[pl.ds(k0, tk), :]
        kp = kpe_ref[pl.ds(k0, tk), :]
        if masked:
            qpos = q0 + lax.broadcasted_iota(jnp.int32, (TQ, tk), 0)
            kpos = k0 + lax.broadcasted_iota(jnp.int32, (TQ, tk), 1)
            visible = (kpos <= qpos)[None]
        for part in range(ATTN_HEAD_PARTS):
            heads = slice(part * hp, (part + 1) * hp)
            rows = slice(part * rp, (part + 1) * rp)
            s = (_dot_nt(ql_ref[heads].reshape(rp, MLA_KV_RANK), kc)
                 + _dot_nt(qp_ref[heads].reshape(rp, MLA_ROPE), kp))
            if masked:
                s = jnp.where(visible, s.reshape(hp, TQ, tk), NEG_INF).reshape(rp, tk)
            _online_softmax_step(s, kc, m_ref.at[rows], l_ref.at[rows], acc_ref.at[rows])

    def body(kb, carry):
        chunk(kb, False)
        return carry

    lax.fori_loop(0, n_full, body, 0)
    chunk(n_full, True)
    o_ref[...] = (acc_ref[...] / l_ref[...]).reshape(H, TQ, MLA_KV_RANK).astype(o_ref.dtype)


def _mla_prompt_attn(q_lat, q_pe, ckv_b, kpe_b, B, S, m_total, tk=MLA_CHUNK):
    H = q_lat.shape[0]
    tk = min(tk, S)
    nq = S // Q_BLOCK
    ckv3 = ckv_b[:B * S].reshape(B, S, MLA_KV_RANK)
    kpe3 = kpe_b[:B * S].reshape(B, S, MLA_ROPE)
    return pl.pallas_call(
        functools.partial(_mla_prompt_attn_kernel, tk=tk),
        grid=(B, nq),
        in_specs=[pl.BlockSpec((H, Q_BLOCK, MLA_KV_RANK), lambda b, i: (0, b * nq + i, 0)),
                  pl.BlockSpec((H, Q_BLOCK, MLA_ROPE), lambda b, i: (0, b * nq + i, 0)),
                  pl.BlockSpec((None, S, MLA_KV_RANK), lambda b, i: (b, 0, 0)),
                  pl.BlockSpec((None, S, MLA_ROPE), lambda b, i: (b, 0, 0))],
        out_specs=pl.BlockSpec((H, Q_BLOCK, MLA_KV_RANK), lambda b, i: (0, b * nq + i, 0)),
        out_shape=jax.ShapeDtypeStruct((H, m_total, MLA_KV_RANK), BF16),
        scratch_shapes=[pltpu.VMEM((H * Q_BLOCK, 1), F32), pltpu.VMEM((H * Q_BLOCK, 1), F32),
                        pltpu.VMEM((H * Q_BLOCK, MLA_KV_RANK), F32)],
        compiler_params=_cparams("parallel", "parallel"),
        name="mla_prompt_attn",
    )(q_lat, q_pe, ckv3, kpe3)


def _mla_sample_attn_kernel(pt_ref, ql_ref, qp_ref, cn_ref, kn_ref, *rest, pages, ds):
    ckv_refs = rest[:pages]
    kpe_refs = rest[pages:2 * pages]
    o_ref, m_ref, l_ref, acc_ref = rest[2 * pages:]
    j = pl.program_id(1)

    @pl.when(j == 0)
    def _():
        m_ref[...] = jnp.full_like(m_ref, NEG_INF)
        l_ref[...] = jnp.zeros_like(l_ref)
        acc_ref[...] = jnp.zeros_like(acc_ref)

    ql = ql_ref[...]
    qp = qp_ref[...]
    kcs = [r[...].astype(BF16) for r in ckv_refs]
    s = jnp.concatenate([_dot_nt(ql, kc) + _dot(qp, kr[...].astype(BF16))
                         for kc, kr in zip(kcs, kpe_refs)], axis=1)
    m_prev = m_ref[...]
    m_new = jnp.maximum(m_prev, jnp.max(s, -1, keepdims=True))
    alpha = jnp.exp(m_prev - m_new)
    p = jnp.exp(s - m_new).astype(BF16)
    l_ref[...] = alpha * l_ref[...] + jnp.sum(p.astype(F32), -1, keepdims=True)
    pv = _dot(p[:, :PAGE_SIZE], kcs[0])
    for i in range(1, pages):
        pv += _dot(p[:, i * PAGE_SIZE:(i + 1) * PAGE_SIZE], kcs[i])
    acc_ref[...] = alpha * acc_ref[...] + pv
    m_ref[...] = m_new

    @pl.when(j == pl.num_programs(1) - 1)
    def _():
        cn = cn_ref[...].astype(BF16)
        s_new = _dot_nt(ql, cn) + _dot_nt(qp, kn_ref[...].astype(BF16))
        t = lax.broadcasted_iota(jnp.int32, s_new.shape, 0) % ds
        u = lax.broadcasted_iota(jnp.int32, s_new.shape, 1)
        s_new = jnp.where(u <= t, s_new, NEG_INF)
        _online_softmax_step(s_new, cn, m_ref, l_ref, acc_ref)
        o_ref[...] = (acc_ref[...] / l_ref[...]).astype(o_ref.dtype)


def _mla_sample_attn(q_lat_s, q_pe_s, cnew, knew, cache_ckv, cache_kpe, li, page_table, pages=DECODE_PAGES):
    DB, R, _ = q_lat_s.shape
    ds = cnew.shape[1]
    n_pages = page_table.shape[1]
    pages = min(pages, n_pages)

    def page_spec(rows, width, i):
        return pl.BlockSpec((None, None, rows, width),
                            lambda b, j, pt: (li, pt[b, j * pages + i], 0, 0))

    kpe_t = jnp.transpose(cache_kpe, (0, 1, 3, 2))
    in_specs = [pl.BlockSpec((None, R, MLA_KV_RANK), lambda b, j, pt: (b, 0, 0)),
                pl.BlockSpec((None, R, MLA_ROPE), lambda b, j, pt: (b, 0, 0)),
                pl.BlockSpec((None, ds, MLA_KV_RANK), lambda b, j, pt: (b, 0, 0)),
                pl.BlockSpec((None, ds, MLA_ROPE), lambda b, j, pt: (b, 0, 0))]
    in_specs += [page_spec(PAGE_SIZE, MLA_KV_RANK, i) for i in range(pages)]
    in_specs += [page_spec(MLA_ROPE, PAGE_SIZE, i) for i in range(pages)]
    return pl.pallas_call(
        functools.partial(_mla_sample_attn_kernel, pages=pages, ds=ds),
        grid_spec=pltpu.PrefetchScalarGridSpec(
            num_scalar_prefetch=1,
            grid=(DB, n_pages // pages),
            in_specs=in_specs,
            out_specs=pl.BlockSpec((None, R, MLA_KV_RANK), lambda b, j, pt: (b, 0, 0)),
            scratch_shapes=[pltpu.VMEM((R, 1), F32), pltpu.VMEM((R, 1), F32),
                            pltpu.VMEM((R, MLA_KV_RANK), F32)]),
        out_shape=jax.ShapeDtypeStruct((DB, R, MLA_KV_RANK), BF16),
        compiler_params=_cparams("parallel", "arbitrary"),
        name="mla_sample_attn",
    )(page_table, q_lat_s, q_pe_s, cnew, knew, *([cache_ckv] * pages), *([kpe_t] * pages))


def _mla_out_kernel(ol_ref, wuv_ref, wo_ref, res_ref, g_ref, b_ref, o_ref):
    acc = None
    for h in range(ol_ref.shape[0]):
        oh = _dot(ol_ref[h], wuv_ref[h]).astype(BF16)
        part = _dot(oh, wo_ref[h * MLA_VDIM:(h + 1) * MLA_VDIM, :])
        acc = part if acc is None else acc + part
    o_ref[...] = _layer_norm(ALPHA * res_ref[...] + acc, g_ref[...], b_ref[...])


def _mla_out(o_lat, w_uv, w_o, res, g, b, tm=ROW_TILE):
    H, M, R = o_lat.shape
    tm = math.gcd(M, tm)
    D = w_o.shape[1]
    return pl.pallas_call(
        _mla_out_kernel,
        grid=(M // tm,),
        in_specs=[pl.BlockSpec((H, tm, R), lambda i: (0, i, 0)),
                  pl.BlockSpec((H, R, MLA_VDIM), lambda i: (0, 0, 0)),
                  pl.BlockSpec((H * MLA_VDIM, D), lambda i: (0, 0)),
                  pl.BlockSpec((tm, D), lambda i: (i, 0)),
                  pl.BlockSpec((1, D), lambda i: (0, 0)),
                  pl.BlockSpec((1, D), lambda i: (0, 0))],
        out_specs=pl.BlockSpec((tm, D), lambda i: (i, 0)),
        out_shape=jax.ShapeDtypeStruct((M, D), F32),
        compiler_params=_cparams("parallel"),
        name="mla_out",
    )(o_lat, w_uv, w_o, res, g.reshape(1, D), b.reshape(1, D))


def _mla_layer(X, dims, cos, sin, cache_ckv, cache_kpe, li, page_table,
               w_down, q_norm, kv_norm, w_uq, w_uk, w_uv, w_o, g, b):
    B, S, DB, DS = dims
    MP = B * S
    M, D = X.shape
    H = MLA_HEADS
    rope_w = w_down[:, MLA_Q_RANK + MLA_KV_RANK:]
    zpad = jnp.zeros((D, LANES - MLA_ROPE), F32)
    w_ext = jnp.concatenate([w_down, zpad, _swap_halves(rope_w), zpad], 1).astype(BF16)
    cq, ckv, kpe, ckv_b, kpe_b = _mla_down(X, w_ext, q_norm, kv_norm, cos, sin)
    wq = w_uq.reshape(MLA_Q_RANK, H, MLA_NOPE + MLA_ROPE)
    hpad = jnp.zeros((MLA_Q_RANK, H, LANES - MLA_ROPE), F32)
    wq_ext = jnp.concatenate([wq, hpad, _swap_halves(wq[..., MLA_NOPE:]), hpad], -1)
    wq_ext = jnp.transpose(wq_ext, (1, 0, 2)).astype(BF16)
    wk = jnp.transpose(w_uk, (1, 0, 2)).astype(BF16)
    q_lat, q_pe = _mla_q(cq, wq_ext, wk, cos, sin)
    o_lat = _mla_prompt_attn(q_lat, q_pe, ckv_b, kpe_b, B, S, M)
    to_seq = lambda a: jnp.transpose(a[:, MP:].reshape(H, DB, DS, -1), (1, 0, 2, 3)).reshape(DB, H * DS, -1)
    o_s = _mla_sample_attn(to_seq(q_lat), to_seq(q_pe),
                           ckv[MP:].reshape(DB, DS, -1), kpe[MP:].reshape(DB, DS, -1),
                           cache_ckv, cache_kpe, li, page_table)
    o_s = jnp.transpose(o_s.reshape(DB, H, DS, -1), (1, 0, 2, 3)).reshape(H, DB * DS, -1)
    o_lat = lax.dynamic_update_slice(o_lat, o_s, (0, MP, 0))
    wuv = jnp.transpose(w_uv, (1, 0, 2)).astype(BF16)
    X = _mla_out(o_lat, wuv, w_o.astype(BF16), X, g, b)
    return X, ckv, kpe


def _np_t5_bucket(rel):
    n = np.maximum(rel, 0)
    exact = N_BUCKETS // 2
    large = exact + (np.log(np.maximum(n, 1) / exact) / math.log(MAX_DISTANCE / exact)
                     * (N_BUCKETS - exact)).astype(np.int64)
    return np.where(n < exact, n, np.minimum(large, N_BUCKETS - 1)).astype(np.int32)


def _bias_tile_kernel(tab_ref, bkt_ref, o_ref):
    bkt = bkt_ref[...]
    for h in range(NSA_HEADS):
        far = tab_ref[N_BUCKETS - 1, h]
        acc = jnp.zeros(bkt.shape, F32)
        for b in range(N_BUCKETS - 1):
            acc = jnp.where(bkt == b, tab_ref[b, h] - far, acc)
        o_ref[h] = acc


def _bias_tiles(table, buckets):
    R, C = buckets.shape
    return pl.pallas_call(
        _bias_tile_kernel,
        in_specs=[pl.BlockSpec(memory_space=pltpu.SMEM), pl.BlockSpec(memory_space=pltpu.VMEM)],
        out_specs=pl.BlockSpec(memory_space=pltpu.VMEM),
        out_shape=jax.ShapeDtypeStruct((NSA_HEADS, R, C), F32),
        name="nsa_bias_tiles",
    )(table, jnp.asarray(buckets))


def _hi_lo(a):
    hi = a.astype(BF16)
    lo = (a - hi.astype(F32)).astype(BF16)
    return hi, lo


def _nsa_in_kernel(x_ref, w_ref, q_ref, kv_ref, kvb_ref, gate_ref):
    h = _dot(x_ref[...].astype(BF16), w_ref[...])
    qw = NSA_HEADS * LANES
    for hd in range(NSA_HEADS):
        q_ref[hd] = (h[:, hd * LANES:(hd + 1) * LANES] * NSA_SCALE).astype(BF16)
    for j in range(6):
        piece = h[:, qw + j * LANES:qw + (j + 1) * LANES]
        kv_ref[j] = piece
        kvb_ref[j] = piece.astype(BF16)
    gate_ref[...] = jax.nn.sigmoid(h[:, qw + 6 * LANES:])


def _nsa_in(x, w_ext, tm=ROW_TILE):
    M, D = x.shape
    tm = math.gcd(M, tm)
    NW = w_ext.shape[1]
    return pl.pallas_call(
        _nsa_in_kernel,
        grid=(M // tm,),
        in_specs=[pl.BlockSpec((tm, D), lambda i: (i, 0)),
                  pl.BlockSpec((D, NW), lambda i: (0, 0))],
        out_specs=[pl.BlockSpec((NSA_HEADS, tm, LANES), lambda i: (0, i, 0)),
                   pl.BlockSpec((6, tm, LANES), lambda i: (0, i, 0)),
                   pl.BlockSpec((6, tm, LANES), lambda i: (0, i, 0)),
                   pl.BlockSpec((tm, LANES), lambda i: (i, 0))],
        out_shape=[jax.ShapeDtypeStruct((NSA_HEADS, M, LANES), BF16),
                   jax.ShapeDtypeStruct((6, M, LANES), F32),
                   jax.ShapeDtypeStruct((6, M, LANES), BF16),
                   jax.ShapeDtypeStruct((M, LANES), F32)],
        compiler_params=_cparams("parallel"),
        name="nsa_in",
    )(x, w_ext)


def _pad_heads_to_group_lanes(w, axis):
    z = jnp.zeros_like(w)
    lo = jnp.concatenate([w, z], axis + 1)
    hi = jnp.concatenate([z, w], axis + 1)
    shape = [1] * w.ndim
    shape[axis] = NSA_HEADS
    first = (jnp.arange(NSA_HEADS) < NSA_HPG).reshape(shape)
    return jnp.where(first, lo, hi)


def _compress_weights(pe, w1, w2):
    eye = jnp.eye(NSA_GROUPS, dtype=F32)
    half = lambda w: jnp.einsum('rde,gh->rgdhe', w, eye).reshape(CMP_STRIDE * LANES, LANES)
    w_big = jnp.concatenate([half(w1[:CMP_STRIDE]), half(w1[CMP_STRIDE:])], 1)
    flat = lambda p: jnp.tile(p[:, None, :], (1, NSA_GROUPS, 1)).reshape(1, CMP_STRIDE * LANES)
    pe2 = jnp.concatenate([flat(pe[:CMP_STRIDE]), flat(pe[CMP_STRIDE:]),
                           jnp.zeros((6, CMP_STRIDE * LANES), F32)], 0)
    return w_big.astype(BF16), pe2.astype(BF16), jnp.kron(eye, w2).astype(BF16)


def _compress_blocks(fs, const, w2):
    nh = fs.shape[0]
    nxt = pltpu.roll(fs[:, LANES:], nh - 1, 0)
    pre = fs[:, :LANES] + nxt + const
    out = _dot((pre * jax.nn.sigmoid(pre)).astype(BF16), w2)
    row = lax.broadcasted_iota(jnp.int32, out.shape, 0)
    return jnp.where(row < nh - 1, out, 0.0).astype(BF16)


def _nsa_compress_kernel(x_ref, w_ref, pe_ref, w2_ref, o_ref, c_ref):
    fs = _dot(x_ref[...], w_ref[...])
    cst = _dot(pe_ref[...], w_ref[...])
    const = cst[0:1, :LANES] + cst[1:2, LANES:]
    c_ref[...] = jnp.broadcast_to(const, c_ref.shape)
    o_ref[...] = _compress_blocks(fs, const, w2_ref[...])


def _nsa_compress(x, w_big, pe2, w2):
    _, B, nh, K = x.shape
    return pl.pallas_call(
        _nsa_compress_kernel,
        grid=(2, B),
        in_specs=[pl.BlockSpec((None, None, nh, K), lambda t, b: (t, b, 0, 0)),
                  pl.BlockSpec((None, K, 2 * LANES), lambda t, b: (t, 0, 0)),
                  pl.BlockSpec((None, 8, K), lambda t, b: (t, 0, 0)),
                  pl.BlockSpec((None, LANES, LANES), lambda t, b: (t, 0, 0))],
        out_specs=[pl.BlockSpec((None, None, nh, LANES), lambda t, b: (t, b, 0, 0)),
                   pl.BlockSpec((None, None, 8, LANES), lambda t, b: (t, b, 0, 0))],
        out_shape=[jax.ShapeDtypeStruct((2, B, nh, LANES), BF16),
                   jax.ShapeDtypeStruct((2, B, 8, LANES), F32)],
        compiler_params=_cparams("parallel", "parallel"),
        name="nsa_compress",
    )(x, w_big, pe2, w2)


def _top_blocks(work, blk, n_pick):
    ns = work.shape[-1]
    sel = jnp.zeros(work.shape, F32)
    for _ in range(n_pick):
        best = jnp.max(work, -1, keepdims=True)
        first = jnp.min(jnp.where(work == best, blk, ns), -1, keepdims=True)
        hit = blk == first
        sel = jnp.where(hit, 1.0, sel)
        work = jnp.where(hit, NEG_INF, work)
    return sel


def _block_importance(p_grp, ns):
    nh = p_grp.shape[-1]
    cc = lax.broadcasted_iota(jnp.int32, (nh, ns), 0)
    jj = lax.broadcasted_iota(jnp.int32, (nh, ns), 1) * (SEL_BLOCK // CMP_STRIDE)
    cover = jnp.where((cc >= jj - 1) & (cc <= jj + 3), 1.0, 0.0).astype(BF16)
    hi = p_grp.astype(BF16)
    r1 = p_grp - hi.astype(F32)
    mid = r1.astype(BF16)
    lo = (r1 - mid.astype(F32)).astype(BF16)
    return _dot(hi, cover) + _dot(mid, cover) + _dot(lo, cover)


def _nsa_prompt_kernel(q_ref, kc_ref, vc_ref, ks_ref, vs_ref, kw_ref, vw_ref, gate_ref, akey_ref, aband_ref,
                       o_ref, m_ref, l_ref, acc_ref, *, tk):
    H, TQ = q_ref.shape[0], q_ref.shape[1]
    R = H * TQ
    G = NSA_GROUPS
    S = ks_ref.shape[0]
    nh = kc_ref.shape[0]
    ns = S // SEL_BLOCK
    qb = pl.program_id(1)
    q0 = qb * TQ
    q = q_ref[...].reshape(R, LANES)
    akey = akey_ref[...].reshape(R, 2 * BAND)
    i32 = jnp.int32

    def key_band(k_first, n_keys):
        d = k_first - q0 + Q_BLOCK + lax.broadcasted_iota(i32, (n_keys, 2 * BAND), 0)
        slot = lax.broadcasted_iota(i32, (n_keys, 2 * BAND), 1) & (BAND - 1)
        return _dot_nt(akey, jnp.where(d == slot, 1.0, 0.0).astype(BF16))

    c_oh = lax.broadcasted_iota(i32, (nh, LANES), 0) - (qb * (Q_BLOCK // CMP_STRIDE) - CMP_BAND_LEAD)
    m_oh = lax.broadcasted_iota(i32, (nh, LANES), 1)
    oh = jnp.where((c_oh == (m_oh & (CMP_BAND - 1))) & (m_oh < 2 * CMP_BAND), 1.0, 0.0).astype(BF16)
    s_c = _dot_nt(q, kc_ref[...]) + _dot_nt(aband_ref[...].reshape(R, LANES), oh)
    c_end = lax.broadcasted_iota(i32, (TQ, nh), 1) * CMP_STRIDE + (CMP_LEN - 1)
    q_pos = q0 + lax.broadcasted_iota(i32, (TQ, nh), 0)
    s_c = s_c.reshape(H, TQ, nh) + jnp.where(c_end <= q_pos, 0.0, NEG_INF)[None]
    m_c = jnp.max(s_c, -1, keepdims=True)
    m_c = jnp.where(m_c == NEG_INF, 0.0, m_c)
    e_c = jnp.exp(s_c - m_c)
    d_c = jnp.sum(e_c, -1, keepdims=True)
    p_c = e_c / jnp.where(d_c > 0, d_c, 1.0)
    o_c = _dot(p_c.reshape(R, nh).astype(BF16), vc_ref[...])

    p_grp = jnp.sum(p_c.reshape(G, NSA_HPG, TQ, nh), axis=1).reshape(G * TQ, nh)
    imp = _block_importance(p_grp, ns)
    blk = lax.broadcasted_iota(i32, (G * TQ, ns), 1)
    cur = (q0 + (lax.broadcasted_iota(i32, (G * TQ, ns), 0) & (TQ - 1))) >> SEL_SHIFT
    forced = (blk == 0) | (blk == cur) | (blk == cur - 1)
    work = jnp.where(forced, jnp.inf, jnp.where(blk <= cur, imp, NEG_INF))
    sel_b = _top_blocks(work, blk, min(N_SEL, ns)).astype(BF16)

    m_ref[...] = jnp.full_like(m_ref, NEG_BIG)
    l_ref[...] = jnp.zeros_like(l_ref)
    acc_ref[...] = jnp.zeros_like(acc_ref)

    def sel_chunk(kb, near):
        k0 = pl.multiple_of(kb * tk, tk)
        s = _dot_nt(q, ks_ref[pl.ds(k0, tk), :])
        eb = lax.broadcasted_iota(i32, (ns, tk), 0)
        ek = (k0 + lax.broadcasted_iota(i32, (ns, tk), 1)) >> SEL_SHIFT
        allow = _dot(sel_b, jnp.where(eb == ek, 1.0, 0.0).astype(BF16))
        mask = (allow - 1.0) * (-NEG_BIG)
        if near:
            s = s + key_band(k0, tk)
            qpos = q0 + (lax.broadcasted_iota(i32, (G * TQ, tk), 0) & (TQ - 1))
            kpos = k0 + lax.broadcasted_iota(i32, (G * TQ, tk), 1)
            mask = jnp.where(kpos <= qpos, mask, NEG_BIG)
        s = (s.reshape(G, NSA_HPG, TQ, tk) + mask.reshape(G, 1, TQ, tk)).reshape(R, tk)
        _online_softmax_step(s, vs_ref[pl.ds(k0, tk), :], m_ref, l_ref, acc_ref)

    n_last = q0 // tk
    n_far = jnp.maximum(q0 - Q_BLOCK, 0) // tk

    def far_body(kb, carry):
        sel_chunk(kb, False)
        return carry

    lax.fori_loop(0, n_far, far_body, 0)

    @pl.when(n_far < n_last)
    def _():
        sel_chunk(n_far, True)

    sel_chunk(n_last, True)
    o_s = acc_ref[...] / l_ref[...]

    nw = WINDOW + TQ
    w0 = pl.multiple_of(jnp.maximum(q0 - WINDOW, 0), TQ)
    s_w = _dot_nt(q, kw_ref[pl.ds(w0, nw), :]) + key_band(w0, nw)
    rel = q0 + lax.broadcasted_iota(i32, (TQ, nw), 0) - (w0 + lax.broadcasted_iota(i32, (TQ, nw), 1))
    s_w = s_w.reshape(H, TQ, nw) + jnp.where((rel >= 0) & (rel < WINDOW), 0.0, NEG_INF)[None]
    e_w = jnp.exp(s_w - jnp.max(s_w, -1, keepdims=True))
    d_w = jnp.sum(e_w, -1, keepdims=True).reshape(R, 1)
    o_w = _dot(e_w.reshape(R, nw).astype(BF16), vw_ref[pl.ds(w0, nw), :]) / d_w

    gates = gate_ref[...]
    for h in range(H):
        rows = slice(h * TQ, (h + 1) * TQ)
        o_h = (gates[:, h:h + 1] * o_c[rows] + gates[:, H + h:H + h + 1] * o_s[rows]
               + gates[:, 2 * H + h:2 * H + h + 1] * o_w[rows])
        o_ref[:, h * LANES:(h + 1) * LANES] = o_h.astype(o_ref.dtype)


def _nsa_prompt_attn(q, cmp_kv, kvb, gates, akey, aband, B, S, m_total):
    H = q.shape[0]
    nq = S // Q_BLOCK
    nh = cmp_kv.shape[2]
    tk = min(SEL_CHUNK, S)
    kv3 = kvb[:, :B * S].reshape(6, B, S, LANES)

    def stream(j):
        return pl.BlockSpec((None, None, S, LANES), lambda b, i: (j, b, 0, 0))

    def cmp(j):
        return pl.BlockSpec((None, None, nh, LANES), lambda b, i: (j, b, 0, 0))

    return pl.pallas_call(
        functools.partial(_nsa_prompt_kernel, tk=tk),
        grid=(B, nq),
        in_specs=[pl.BlockSpec((H, Q_BLOCK, LANES), lambda b, i: (0, b * nq + i, 0)),
                  cmp(0), cmp(1), stream(2), stream(3), stream(4), stream(5),
                  pl.BlockSpec((Q_BLOCK, LANES), lambda b, i: (b * nq + i, 0)),
                  pl.BlockSpec((H, Q_BLOCK, 2 * BAND), lambda b, i: (0, 0, 0)),
                  pl.BlockSpec((H, Q_BLOCK, LANES), lambda b, i: (0, 0, 0))],
        out_specs=pl.BlockSpec((Q_BLOCK, H * LANES), lambda b, i: (b * nq + i, 0)),
        out_shape=jax.ShapeDtypeStruct((m_total, H * LANES), BF16),
        scratch_shapes=[pltpu.VMEM((H * Q_BLOCK, 1), F32), pltpu.VMEM((H * Q_BLOCK, 1), F32),
                        pltpu.VMEM((H * Q_BLOCK, LANES), F32)],
        compiler_params=_cparams("parallel", "parallel"),
        name="nsa_prompt_attn",
    )(q, cmp_kv, cmp_kv, kv3, kv3, kv3, kv3, gates, akey, aband)


def _feature_major_pages(cache):
    L, POOL = cache.shape[:2]
    return jnp.transpose(cache, (0, 1, 3, 4, 2)).reshape(L, POOL, LANES, cache.shape[2])


def _nsa_fs_kernel(pt_ref, w_ref, *rest):
    page_refs, (o_ref, rows_ref) = rest[:-2], rest[-2:]
    for i, r in enumerate(page_refs):
        rows_ref[i * PAGE_SIZE:(i + 1) * PAGE_SIZE, :] = r[...].T
    n_half = o_ref.shape[0]
    acc = None
    for pos in range(CMP_STRIDE):
        x = rows_ref[pl.ds(pos, n_half, stride=CMP_STRIDE), :].astype(BF16)
        part = _dot(x, w_ref[pos])
        acc = part if acc is None else acc + part
    o_ref[...] = acc


def _nsa_fs_sample(cache, li, page_table, w_big, pages=16):
    DB, n_pages = page_table.shape
    pages = min(pages, n_pages)
    hb = PAGE_SIZE // CMP_STRIDE
    view = _feature_major_pages(cache)
    w_pos = w_big.reshape(CMP_STRIDE, LANES, 2 * LANES)

    def page_spec(i):
        return pl.BlockSpec((None, None, LANES, PAGE_SIZE), lambda b, j, pt: (li, pt[b, j * pages + i], 0, 0))

    return pl.pallas_call(
        _nsa_fs_kernel,
        grid_spec=pltpu.PrefetchScalarGridSpec(
            num_scalar_prefetch=1,
            grid=(DB, n_pages // pages),
            in_specs=[pl.BlockSpec((CMP_STRIDE, LANES, 2 * LANES), lambda b, j, pt: (0, 0, 0))]
                     + [page_spec(i) for i in range(pages)],
            out_specs=pl.BlockSpec((None, pages * hb, 2 * LANES), lambda b, j, pt: (b, j, 0)),
            scratch_shapes=[pltpu.VMEM((pages * PAGE_SIZE, LANES), F32)]),
        out_shape=jax.ShapeDtypeStruct((DB, n_pages * hb, 2 * LANES), F32),
        compiler_params=_cparams("parallel", "parallel"),
        name="nsa_fs_sample",
    )(page_table, w_pos, *([view] * pages))


def _nsa_sample_kernel(pt_ref, q_ref, fsk_ref, fsv_ref, const_ref, w2_ref, skw_ref, svw_ref, new_ref, gate_ref,
                       tile_ref, *rest, pages, ds, past):
    ks_refs = rest[:pages]
    vs_refs = rest[pages:2 * pages]
    o_ref, sel_ref, oc_ref, m_ref, l_ref, acc_ref = rest[2 * pages:]
    j = pl.program_id(1)
    n_steps = pl.num_programs(1)
    H, G = NSA_HEADS, NSA_GROUPS
    R = ds * H
    nh = fsk_ref.shape[0]
    ns = past // SEL_BLOCK
    win = skw_ref.shape[1]
    i32 = jnp.int32
    q = q_ref[...]
    tok = lax.broadcasted_iota(i32, (R, 1), 0) >> HEAD_SHIFT

    @pl.when(j == 0)
    def _():
        kcb = _compress_blocks(fsk_ref[...], const_ref[0], w2_ref[0])
        vcb = _compress_blocks(fsv_ref[...], const_ref[1], w2_ref[1])
        s_c = _dot_nt(q, kcb) + tile_ref[:, TILE_CMP:TILE_CMP + nh]
        c_end = lax.broadcasted_iota(i32, (R, nh), 1) * CMP_STRIDE + (CMP_LEN - 1)
        s_c = jnp.where(c_end <= past + tok, s_c, NEG_INF)
        m_c = jnp.max(s_c, -1, keepdims=True)
        m_c = jnp.where(m_c == NEG_INF, 0.0, m_c)
        e_c = jnp.exp(s_c - m_c)
        d_c = jnp.sum(e_c, -1, keepdims=True)
        p_c = e_c / jnp.where(d_c > 0, d_c, 1.0)
        oc_ref[...] = _dot(p_c.astype(BF16), vcb)
        p_grp = jnp.sum(p_c.reshape(ds * G, NSA_HPG, nh), axis=1)
        imp = _block_importance(p_grp, ns)
        blk = lax.broadcasted_iota(i32, (ds * G, ns), 1)
        work = jnp.where((blk == 0) | (blk == ns - 1), jnp.inf, imp)
        sel_ref[...] = _top_blocks(work, blk, min(N_SEL - 1, ns))
        m_ref[...] = jnp.full_like(m_ref, NEG_BIG)
        l_ref[...] = jnp.zeros_like(l_ref)
        acc_ref[...] = jnp.zeros_like(acc_ref)

    def online(s, v_list, feature_major):
        pv_dot = _dot_nt if feature_major else _dot
        m_prev = m_ref[...]
        m_new = jnp.maximum(m_prev, jnp.max(s, -1, keepdims=True))
        alpha = jnp.exp(m_prev - m_new)
        p = jnp.exp(s - m_new)
        l_ref[...] = alpha * l_ref[...] + jnp.sum(p, -1, keepdims=True)
        pb = p.astype(BF16)
        width = s.shape[1] // len(v_list)
        pv = pv_dot(pb[:, :width], v_list[0])
        for i in range(1, len(v_list)):
            pv += pv_dot(pb[:, i * width:(i + 1) * width], v_list[i])
        acc_ref[...] = alpha * acc_ref[...] + pv
        m_ref[...] = m_new

    nk = pages * PAGE_SIZE
    is_last = (j == n_steps - 1).astype(F32)
    s_parts = []
    for i in range(pages):
        s_i = _dot(q, ks_refs[i][...].astype(BF16))
        if i == pages - 1:
            s_i = s_i + is_last * tile_ref[:, TILE_LAST:TILE_LAST + PAGE_SIZE]
        s_parts.append(s_i)
    s = jnp.concatenate(s_parts, axis=1)
    eb = lax.broadcasted_iota(i32, (ns, nk), 0)
    ek = (j * nk + lax.broadcasted_iota(i32, (ns, nk), 1)) >> SEL_SHIFT
    allow = _dot(sel_ref[...].astype(BF16), jnp.where(eb == ek, 1.0, 0.0).astype(BF16))
    mask = (allow - 1.0) * (-NEG_BIG)
    s = (s.reshape(ds * G, NSA_HPG, nk) + mask[:, None, :]).reshape(R, nk)
    online(s, [r[...].astype(BF16) for r in vs_refs], True)

    @pl.when(j == n_steps - 1)
    def _():
        new_b = new_ref[...].astype(BF16)
        u = lax.broadcasted_iota(i32, (R, ds), 1)
        nb = tile_ref[:, TILE_NEW:TILE_NEW + ds] + jnp.where(u <= tok, 0.0, NEG_INF)
        online(_dot_nt(q, new_b[0]) + nb, [new_b[1]], False)
        o_s = acc_ref[...] / l_ref[...]
        idx = lax.broadcasted_iota(i32, (R, win), 1)
        s1 = _dot(q, skw_ref[...].astype(BF16)) + tile_ref[:, TILE_WIN:TILE_WIN + win]
        s1 = jnp.where(idx > tok + (win - WINDOW), s1, NEG_INF)
        s2 = _dot_nt(q, new_b[2]) + nb
        m_w = jnp.maximum(jnp.max(s1, -1, keepdims=True), jnp.max(s2, -1, keepdims=True))
        e1 = jnp.exp(s1 - m_w)
        e2 = jnp.exp(s2 - m_w)
        d_w = jnp.sum(e1, -1, keepdims=True) + jnp.sum(e2, -1, keepdims=True)
        o_w = (_dot_nt(e1.astype(BF16), svw_ref[...].astype(BF16)) + _dot(e2.astype(BF16), new_b[3])) / d_w
        g = gate_ref[...]
        o_ref[...] = (g[0] * oc_ref[...] + g[1] * o_s + g[2] * o_w).astype(o_ref.dtype)


def _nsa_sample_attn(q_s, fs_k, fs_v, const, w2, state_kw, state_vw, new4, gates_s, tiles,
                     cache_ks, cache_vs, li, page_table, pages=DECODE_PAGES):
    DB, R, _ = q_s.shape
    ds = new4.shape[2]
    n_pages = page_table.shape[1]
    pages = min(pages, n_pages)
    past = n_pages * PAGE_SIZE
    nh = fs_k.shape[1]
    win = state_kw.shape[2]
    L, POOL = cache_ks.shape[:2]
    ks_view = _feature_major_pages(cache_ks)
    vs_view = _feature_major_pages(cache_vs)
    skw = _feature_major_pages(state_kw)
    svw = _feature_major_pages(state_vw)

    def page_spec(i):
        return pl.BlockSpec((None, None, LANES, PAGE_SIZE), lambda b, j, pt: (li, pt[b, j * pages + i], 0, 0))

    per_seq = lambda *shape: pl.BlockSpec((None,) + shape, lambda b, j, pt: (b,) + (0,) * len(shape))
    in_specs = [per_seq(R, LANES), per_seq(nh, 2 * LANES), per_seq(nh, 2 * LANES),
                pl.BlockSpec((2, 1, LANES), lambda b, j, pt: (0, 0, 0)),
                pl.BlockSpec((2, LANES, LANES), lambda b, j, pt: (0, 0, 0)),
                pl.BlockSpec((None, None, LANES, win), lambda b, j, pt: (li, b, 0, 0)),
                pl.BlockSpec((None, None, LANES, win), lambda b, j, pt: (li, b, 0, 0)),
                per_seq(4, ds, LANES), per_seq(3, R, 1),
                pl.BlockSpec(tiles.shape, lambda b, j, pt: (0, 0))]
    in_specs += [page_spec(i) for i in range(pages)] * 2
    return pl.pallas_call(
        functools.partial(_nsa_sample_kernel, pages=pages, ds=ds, past=past),
        grid_spec=pltpu.PrefetchScalarGridSpec(
            num_scalar_prefetch=1,
            grid=(DB, n_pages // pages),
            in_specs=in_specs,
            out_specs=per_seq(R, LANES),
            scratch_shapes=[pltpu.VMEM((ds * NSA_GROUPS, past // SEL_BLOCK), F32),
                            pltpu.VMEM((R, LANES), F32),
                            pltpu.VMEM((R, 1), F32), pltpu.VMEM((R, 1), F32), pltpu.VMEM((R, LANES), F32)]),
        out_shape=jax.ShapeDtypeStruct((DB, R, LANES), BF16),
        compiler_params=_cparams("parallel", "arbitrary"),
        name="nsa_sample_attn",
    )(page_table, q_s, fs_k, fs_v, const, w2, skw, svw, new4, gates_s, tiles,
      *([ks_view] * pages), *([vs_view] * pages))


def _nsa_bias_inputs(table, ds, past, win):
    i = np.arange(Q_BLOCK)[:, None]
    rel_key = i - np.arange(BAND)[None, :] + Q_BLOCK
    rel_cmp = i - CMP_STRIDE * (np.arange(CMP_BAND)[None, :] - CMP_BAND_LEAD) - (CMP_LEN - 1)
    sheet_p = np.concatenate([rel_key, rel_cmp, np.zeros((Q_BLOCK, LANES - CMP_BAND), np.int64)], 1)
    tp = _bias_tiles(table, _np_t5_bucket(sheet_p))
    hi, lo = _hi_lo(tp[:, :, :BAND])
    akey = jnp.concatenate([hi, lo], -1)
    hi, lo = _hi_lo(tp[:, :, BAND:BAND + CMP_BAND])
    aband = jnp.concatenate([hi, lo, jnp.zeros((NSA_HEADS, Q_BLOCK, LANES - 2 * CMP_BAND), BF16)], -1)
    t = np.arange(ds)[:, None]
    nh = past // CMP_STRIDE
    assert nh <= TILE_WIN - TILE_CMP and win <= TILE_LAST - TILE_WIN and ds <= SEL_BLOCK
    pad = lambda a, w: np.concatenate([a, np.zeros((ds, w - a.shape[1]), np.int64)], 1)
    rel_c = past + t - CMP_STRIDE * np.arange(nh)[None, :] - (CMP_LEN - 1)
    rel_w = t + win - np.arange(win)[None, :]
    rel_l = t + PAGE_SIZE - np.arange(PAGE_SIZE)[None, :]
    rel_n = t - np.arange(ds)[None, :]
    sheet_s = np.concatenate([pad(rel_c, TILE_WIN - TILE_CMP), pad(rel_w, TILE_LAST - TILE_WIN),
                              pad(rel_l, TILE_NEW - TILE_LAST), pad(rel_n, LANES)], 1)
    ts = _bias_tiles(table, _np_t5_bucket(sheet_s))
    tiles = jnp.transpose(ts, (1, 0, 2)).reshape(ds * NSA_HEADS, -1)
    return akey, aband, tiles


def _nsa_layer(X, dims, li, bias_inputs, cache_kc, cache_vc, cache_ks, cache_vs, state_kw, state_vw, page_table,
               w_in, pe_k, w1_k, w2_k, pe_v, w1_v, w2_v, w_o, g, b):
    B, S, DB, DS = dims
    MP = B * S
    M, D = X.shape
    H = NSA_HEADS
    akey, aband, tiles = bias_inputs
    qw = H * NSA_DH
    kvw = 6 * NSA_GROUPS * NSA_DH
    wq = _pad_heads_to_group_lanes(w_in[:, :qw].reshape(D, H, NSA_DH), 1).reshape(D, H * LANES)
    w_ext = jnp.concatenate([wq, w_in[:, qw:], jnp.zeros((D, LANES - (w_in.shape[1] - qw - kvw)), F32)], 1)
    q, kv, kvb, gates = _nsa_in(X, w_ext.astype(BF16))
    wk_big, pe2_k, w2k = _compress_weights(pe_k, w1_k, w2_k)
    wv_big, pe2_v, w2v = _compress_weights(pe_v, w1_v, w2_v)
    w2 = jnp.stack([w2k, w2v])
    nh = S // CMP_STRIDE
    cmp_kv, const = _nsa_compress(kvb[:2, :MP].reshape(2, B, nh, CMP_STRIDE * LANES),
                                  jnp.stack([wk_big, wv_big]), jnp.stack([pe2_k, pe2_v]), w2)
    o = _nsa_prompt_attn(q, cmp_kv, kvb, gates, akey, aband, B, S, M)
    q_s = jnp.transpose(q[:, MP:].reshape(H, DB, DS, LANES), (1, 2, 0, 3)).reshape(DB, DS * H, LANES)
    fs_k = _nsa_fs_sample(cache_kc, li, page_table, wk_big)
    fs_v = _nsa_fs_sample(cache_vc, li, page_table, wv_big)
    new4 = jnp.transpose(kv[2:, MP:].reshape(4, DB, DS, LANES), (1, 0, 2, 3))
    gates_s = jnp.transpose(gates[MP:, :3 * H].reshape(DB, DS, 3, H), (0, 2, 1, 3)).reshape(DB, 3, DS * H, 1)
    o_s = _nsa_sample_attn(q_s, fs_k, fs_v, const[:, 0, :1], w2, state_kw, state_vw, new4, gates_s, tiles,
                           cache_ks, cache_vs, li, page_table)
    o = lax.dynamic_update_slice(o, o_s.reshape(DB * DS, H * LANES), (MP, 0))
    wo = _pad_heads_to_group_lanes(w_o.reshape(H, NSA_DH, D), 0).reshape(H * LANES, D)
    X = _mm_ln(o, wo.astype(BF16), X, g, b, name="nsa_out")
    split = lambda a: (a[:MP].reshape(B, S, NSA_GROUPS, NSA_DH), a[MP:].reshape(DB, DS, NSA_GROUPS, NSA_DH))
    pieces = [split(kv[j]) for j in range(6)]
    keep = min(WINDOW, S)
    outs_p = [p for p, _ in pieces[:4]] + [pieces[4][0][:, S - keep:], pieces[5][0][:, S - keep:]]
    outs_s = [s for _, s in pieces[:4]] + [jnp.concatenate([state_kw[li], pieces[4][1]], 1)[:, DS:],
                                           jnp.concatenate([state_vw[li], pieces[5][1]], 1)[:, DS:]]
    return X, outs_p, outs_s


def kernel(x_prompt, x_sample, cache_mla_ckv, cache_mla_kpe, cache_nsa_kc, cache_nsa_vc, cache_nsa_ks, cache_nsa_vs, state_nsa_kw, state_nsa_vw, cache_mem_k, cache_mem_v, page_table, mem_prompt, ln_g, ln_b, mla_w_down, mla_q_norm, mla_kv_norm, mla_w_uq, mla_w_uk, mla_w_uv, mla_w_o, nsa_w_in, nsa_cmp_pe_k, nsa_cmp_w1_k, nsa_cmp_w2_k, nsa_cmp_pe_v, nsa_cmp_w1_v, nsa_cmp_w2_v, nsa_w_o, rel_bias, xa_w_q, xa_w_kv, xa_w_o, ffn_w_gu, ffn_w_down, moe_router, moe_w_gu, moe_w_down):
    B, S, D = x_prompt.shape
    DB, DS, _ = x_sample.shape
    dims = (B, S, DB, DS)
    MP, MS = B * S, DB * DS
    past = page_table.shape[1] * PAGE_SIZE
    n_mem = mem_prompt.shape[1]
    depth = ln_g.shape[0]
    HD = MEM_HEADS * MEM_DH

    X = jnp.concatenate([x_prompt.reshape(MP, D), x_sample.reshape(MS, D)], 0)
    pos = jnp.concatenate([jnp.tile(jnp.arange(S, dtype=jnp.int32), B),
                           jnp.tile(past + jnp.arange(DS, dtype=jnp.int32), DB)])
    cos, sin = _rope_tables(pos)
    mem_rows = mem_prompt.reshape(B * n_mem, D)
    bias_inputs = _nsa_bias_inputs(rel_bias, DS, past, state_nsa_kw.shape[2])

    outs_p = {k: [] for k in ("ckv", "kpe", "kc", "vc", "ks", "vs", "kw", "vw", "mk", "mv")}
    outs_s = {k: [] for k in ("ckv", "kpe", "kc", "vc", "ks", "vs", "kw", "vw")}
    for i in range(depth):
        li = i // 2
        if i % 2 == 0:
            X, ckv, kpe = _mla_layer(X, dims, cos, sin, cache_mla_ckv, cache_mla_kpe, li, page_table,
                                     mla_w_down[li], mla_q_norm[li], mla_kv_norm[li], mla_w_uq[li],
                                     mla_w_uk[li], mla_w_uv[li], mla_w_o[li], ln_g[i, 0], ln_b[i, 0])
            outs_p["ckv"].append(ckv[:MP].reshape(B, S, -1))
            outs_p["kpe"].append(kpe[:MP].reshape(B, S, -1))
            outs_s["ckv"].append(ckv[MP:].reshape(DB, DS, -1))
            outs_s["kpe"].append(kpe[MP:].reshape(DB, DS, -1))
        else:
            X, o_p, o_s = _nsa_layer(X, dims, li, bias_inputs, cache_nsa_kc, cache_nsa_vc, cache_nsa_ks,
                                     cache_nsa_vs, state_nsa_kw, state_nsa_vw, page_table,
                                     nsa_w_in[li], nsa_cmp_pe_k[li], nsa_cmp_w1_k[li], nsa_cmp_w2_k[li],
                                     nsa_cmp_pe_v[li], nsa_cmp_w1_v[li], nsa_cmp_w2_v[li], nsa_w_o[li],
                                     ln_g[i, 0], ln_b[i, 0])
            for k, v in zip(("kc", "vc", "ks", "vs", "kw", "vw"), o_p):
                outs_p[k].append(v)
            for k, v in zip(("kc", "vc", "ks", "vs", "kw", "vw"), o_s):
                outs_s[k].append(v)
        kv = _mm(mem_rows, xa_w_kv[i].astype(BF16), name="mem_kv").reshape(B, n_mem, 2, MEM_HEADS, MEM_DH)
        mk, mv = kv[:, :, 0], kv[:, :, 1]
        outs_p["mk"].append(mk)
        outs_p["mv"].append(mv)
        q = _mm(X, xa_w_q[i].astype(BF16), out_dtype=BF16, scale=MEM_SCALE, name="xattn_q")
        o_p = _xattn(q[:MP].reshape(B, S, HD), mk.reshape(B, n_mem, HD), mv.reshape(B, n_mem, HD),
                     rows=min(ROW_TILE, S), group=1)
        o_s = _xattn(q[MP:].reshape(DB, DS, HD), cache_mem_k[i].reshape(DB, n_mem, HD),
                     cache_mem_v[i].reshape(DB, n_mem, HD), rows=DS, group=min(8, DB))
        o = jnp.concatenate([o_p.reshape(MP, HD), o_s.reshape(MS, HD)], 0)
        X = _mm_ln(o, xa_w_o[i].astype(BF16), X, ln_g[i, 1], ln_b[i, 1], name="xattn_out")
        if i % 2 == 0:
            X = _ffn_ln(X, ffn_w_gu[li][None].astype(BF16), ffn_w_down[li][None].astype(BF16),
                        ln_g[i, 2], ln_b[i, 2])
        else:
            comb = _router(X, moe_router[li])
            X = _ffn_ln(X, moe_w_gu[li].astype(BF16), moe_w_down[li].astype(BF16),
                        ln_g[i, 2], ln_b[i, 2], comb=comb)

    st = jnp.stack
    return (X[:MP].reshape(B, S, D), X[MP:].reshape(DB, DS, D),
            st(outs_p["ckv"]), st(outs_p["kpe"]), st(outs_p["kc"]), st(outs_p["vc"]),
            st(outs_p["ks"]), st(outs_p["vs"]), st(outs_p["kw"]), st(outs_p["vw"]),
            st(outs_p["mk"]), st(outs_p["mv"]),
            st(outs_s["ckv"]), st(outs_s["kpe"]), st(outs_s["kc"]), st(outs_s["vc"]),
            st(outs_s["ks"]), st(outs_s["vs"]), st(outs_s["kw"]), st(outs_s["vw"]))
```

```python
import functools
import math

import jax
import jax.numpy as jnp
import numpy as np
from jax import lax
from jax.experimental import pallas as pl
from jax.experimental.pallas import tpu as pltpu

F32 = jnp.float32
BF16 = jnp.bfloat16

DEPTH = 4
PAGE_SIZE = 128
MLA_HEADS = 16
MLA_Q_RANK = 384
MLA_KV_RANK = 256
MLA_NOPE = 128
MLA_ROPE = 64
MLA_VDIM = 128
MLA_SCALE = (MLA_NOPE + MLA_ROPE) ** -0.5
ROPE_THETA = 10000.0
NSA_HEADS = 16
NSA_GROUPS = 2
NSA_HPG = NSA_HEADS // NSA_GROUPS
NSA_DH = 64
NSA_SCALE = NSA_DH ** -0.5
CMP_LEN = 32
CMP_STRIDE = 16
SEL_BLOCK = 64
N_SEL = 16
WINDOW = 512
N_BUCKETS = 32
MAX_DISTANCE = 128
MEM_HEADS = 4
MEM_DH = 128
MEM_SCALE = MEM_DH ** -0.5
N_EXPERTS = 8
ALPHA = (2 * DEPTH) ** 0.25
LN_EPS = 1e-5
RMS_EPS = 1e-6
Q_BLOCK = 128

LANES = 128
VMEM_LIMIT = 56 * 1024 * 1024
ROW_TILE = 512
NEG_INF = float("-inf")
NEG_BIG = -1e30

ATTN_HEAD_PARTS = 8
MLA_CHUNK = 1024

SEL_CHUNK = 1024
DECODE_PAGES = 32
BAND = 2 * Q_BLOCK
CMP_BAND = 16
CMP_BAND_LEAD = 9
SEL_SHIFT = SEL_BLOCK.bit_length() - 1
HEAD_SHIFT = NSA_HEADS.bit_length() - 1
TILE_CMP, TILE_WIN, TILE_LAST, TILE_NEW = 0, 4 * LANES, 8 * LANES, 9 * LANES


def _cparams(*sem):
    return pltpu.CompilerParams(dimension_semantics=tuple(sem), vmem_limit_bytes=VMEM_LIMIT)


def _dot(a, b):
    return jnp.dot(a, b, preferred_element_type=F32)


def _dot_nt(a, b):
    return lax.dot_general(a, b, (((1,), (1,)), ((), ())), preferred_element_type=F32)


def _layer_norm(h, g, b):
    mu = jnp.mean(h, -1, keepdims=True)
    d = h - mu
    var = jnp.mean(d * d, -1, keepdims=True)
    return d * lax.rsqrt(var + LN_EPS) * g + b


def _rms_norm(x, g):
    return x * lax.rsqrt(jnp.mean(x * x, -1, keepdims=True) + RMS_EPS) * g


def _mm_kernel(x_ref, w_ref, o_ref, *, scale):
    acc = _dot(x_ref[...].astype(BF16), w_ref[...])
    if scale is not None:
        acc = acc * scale
    o_ref[...] = acc.astype(o_ref.dtype)


def _mm(x, w, out_dtype=F32, scale=None, tm=ROW_TILE, name="mm"):
    M, K = x.shape
    N = w.shape[1]
    tm = math.gcd(M, tm)
    return pl.pallas_call(
        functools.partial(_mm_kernel, scale=scale),
        grid=(M // tm,),
        in_specs=[pl.BlockSpec((tm, K), lambda i: (i, 0)),
                  pl.BlockSpec((K, N), lambda i: (0, 0))],
        out_specs=pl.BlockSpec((tm, N), lambda i: (i, 0)),
        out_shape=jax.ShapeDtypeStruct((M, N), out_dtype),
        compiler_params=_cparams("parallel"),
        name=name,
    )(x, w)


def _mm_ln_kernel(a_ref, w_ref, res_ref, g_ref, b_ref, o_ref):
    acc = _dot(a_ref[...].astype(BF16), w_ref[...])
    o_ref[...] = _layer_norm(ALPHA * res_ref[...] + acc, g_ref[...], b_ref[...])


def _mm_ln(a, w, res, g, b, tm=ROW_TILE, name="mm_ln"):
    M, K = a.shape
    D = w.shape[1]
    tm = math.gcd(M, tm)
    return pl.pallas_call(
        _mm_ln_kernel,
        grid=(M // tm,),
        in_specs=[pl.BlockSpec((tm, K), lambda i: (i, 0)),
                  pl.BlockSpec((K, D), lambda i: (0, 0)),
                  pl.BlockSpec((tm, D), lambda i: (i, 0)),
                  pl.BlockSpec((1, D), lambda i: (0, 0)),
                  pl.BlockSpec((1, D), lambda i: (0, 0))],
        out_specs=pl.BlockSpec((tm, D), lambda i: (i, 0)),
        out_shape=jax.ShapeDtypeStruct((M, D), F32),
        compiler_params=_cparams("parallel"),
        name=name,
    )(a, w, res, g.reshape(1, D), b.reshape(1, D))


def _router_kernel(x_ref, r_ref, o_ref):
    logits = jnp.dot(x_ref[...], r_ref[...], preferred_element_type=F32,
                     precision=lax.Precision.HIGHEST)
    lane = lax.broadcasted_iota(jnp.int32, logits.shape, 1)
    logits = jnp.where(lane < N_EXPERTS, logits, NEG_INF)
    v1 = jnp.max(logits, -1, keepdims=True)
    i1 = jnp.min(jnp.where(logits == v1, lane, LANES), -1, keepdims=True)
    rest = jnp.where(lane == i1, NEG_INF, logits)
    v2 = jnp.max(rest, -1, keepdims=True)
    i2 = jnp.min(jnp.where(rest == v2, lane, LANES), -1, keepdims=True)
    e2 = jnp.exp(v2 - v1)
    w1 = 1.0 / (1.0 + e2)
    w2 = e2 / (1.0 + e2)
    o_ref[...] = jnp.where(lane == i1, w1, 0.0) + jnp.where(lane == i2, w2, 0.0)


def _router(x, router, tm=ROW_TILE):
    M, D = x.shape
    tm = math.gcd(M, tm)
    r = jnp.pad(router, ((0, 0), (0, LANES - router.shape[1])))
    return pl.pallas_call(
        _router_kernel,
        grid=(M // tm,),
        in_specs=[pl.BlockSpec((tm, D), lambda i: (i, 0)),
                  pl.BlockSpec((D, LANES), lambda i: (0, 0))],
        out_specs=pl.BlockSpec((tm, LANES), lambda i: (i, 0)),
        out_shape=jax.ShapeDtypeStruct((M, LANES), F32),
        compiler_params=_cparams("parallel"),
        name="moe_router",
    )(x, r)


def _ffn_kernel(*refs, n_e, n_f, routed):
    if routed:
        x_ref, comb_ref, wg_ref, wu_ref, wd_ref, g_ref, b_ref, o_ref, acc_ref, xb_ref = refs
    else:
        x_ref, wg_ref, wu_ref, wd_ref, g_ref, b_ref, o_ref, acc_ref, xb_ref = refs
    e = pl.program_id(1)
    f = pl.program_id(2)

    @pl.when((e == 0) & (f == 0))
    def _():
        acc_ref[...] = jnp.zeros_like(acc_ref)
        xb_ref[...] = x_ref[...].astype(BF16)

    xb = xb_ref[...]
    gate = _dot(xb, wg_ref[...])
    up = _dot(xb, wu_ref[...])
    act = gate * jax.nn.sigmoid(gate) * up
    if routed:
        comb = comb_ref[...]
        lane = lax.broadcasted_iota(jnp.int32, comb.shape, 1)
        act = act * jnp.sum(jnp.where(lane == e, comb, 0.0), -1, keepdims=True)
    acc_ref[...] += _dot(act.astype(BF16), wd_ref[...])

    @pl.when((e == n_e - 1) & (f == n_f - 1))
    def _():
        o_ref[...] = _layer_norm(ALPHA * x_ref[...] + acc_ref[...], g_ref[...], b_ref[...])


def _ffn_ln(x, w_gu, w_down, g, b, comb=None, tm=ROW_TILE, tf=1408):
    M, D = x.shape
    tm = math.gcd(M, tm)
    E, F = w_down.shape[0], w_down.shape[1]
    n_f = F // tf
    routed = comb is not None
    in_specs = [pl.BlockSpec((tm, D), lambda i, e, f: (i, 0))]
    args = [x]
    if routed:
        in_specs.append(pl.BlockSpec((tm, LANES), lambda i, e, f: (i, 0)))
        args.append(comb)
    in_specs += [pl.BlockSpec((None, D, tf), lambda i, e, f: (e, 0, f)),
                 pl.BlockSpec((None, D, tf), lambda i, e, f: (e, 0, f + n_f)),
                 pl.BlockSpec((None, tf, D), lambda i, e, f: (e, f, 0)),
                 pl.BlockSpec((1, D), lambda i, e, f: (0, 0)),
                 pl.BlockSpec((1, D), lambda i, e, f: (0, 0))]
    args += [w_gu, w_gu, w_down, g.reshape(1, D), b.reshape(1, D)]
    return pl.pallas_call(
        functools.partial(_ffn_kernel, n_e=E, n_f=n_f, routed=routed),
        grid=(M // tm, E, n_f),
        in_specs=in_specs,
        out_specs=pl.BlockSpec((tm, D), lambda i, e, f: (i, 0)),
        out_shape=jax.ShapeDtypeStruct((M, D), F32),
        scratch_shapes=[pltpu.VMEM((tm, D), F32), pltpu.VMEM((tm, D), BF16)],
        compiler_params=_cparams("parallel", "arbitrary", "arbitrary"),
        name="ffn_ln",
    )(*args)


def _xattn_kernel(q_ref, k_ref, v_ref, o_ref):
    for gi in range(q_ref.shape[0]):
        for h in range(MEM_HEADS):
            sl = slice(h * MEM_DH, (h + 1) * MEM_DH)
            qh = q_ref[gi, :, sl]
            kh = k_ref[gi, :, sl].astype(BF16)
            vh = v_ref[gi, :, sl].astype(BF16)
            s = _dot_nt(qh, kh)
            m = jnp.max(s, -1, keepdims=True)
            p = jnp.exp(s - m)
            p = p / jnp.sum(p, -1, keepdims=True)
            o_ref[gi, :, sl] = _dot(p.astype(BF16), vh).astype(o_ref.dtype)


def _xattn(q, k, v, rows, group):
    NB, R, HD = q.shape
    NM = k.shape[1]
    return pl.pallas_call(
        _xattn_kernel,
        grid=(NB // group, R // rows),
        in_specs=[pl.BlockSpec((group, rows, HD), lambda b, i: (b, i, 0)),
                  pl.BlockSpec((group, NM, HD), lambda b, i: (b, 0, 0)),
                  pl.BlockSpec((group, NM, HD), lambda b, i: (b, 0, 0))],
        out_specs=pl.BlockSpec((group, rows, HD), lambda b, i: (b, i, 0)),
        out_shape=jax.ShapeDtypeStruct((NB, R, HD), BF16),
        compiler_params=_cparams("parallel", "parallel"),
        name="xattn",
    )(q, k, v)


def _rope_tables(pos):
    half = MLA_ROPE // 2
    inv = ROPE_THETA ** (-jnp.arange(half, dtype=F32) / half)
    ang = pos.astype(F32)[:, None] * inv[None, :]
    c, s = jnp.cos(ang), jnp.sin(ang)
    return jnp.concatenate([c, c], -1), jnp.concatenate([-s, s], -1)


def _swap_halves(w):
    half = w.shape[-1] // 2
    return jnp.concatenate([w[..., half:], w[..., :half]], -1)


def _mla_down_kernel(x_ref, w_ref, qn_ref, kvn_ref, cos_ref, sin_ref,
                     cq_ref, ckv_ref, kpe_ref, ckvb_ref, kpeb_ref):
    down = _dot(x_ref[...].astype(BF16), w_ref[...])
    cq_ref[...] = _rms_norm(down[:, :MLA_Q_RANK], qn_ref[...]).astype(BF16)
    ckv = _rms_norm(down[:, MLA_Q_RANK:MLA_Q_RANK + MLA_KV_RANK], kvn_ref[...])
    ckv_ref[...] = ckv
    ckvb_ref[...] = ckv.astype(BF16)
    ra = down[:, 640:640 + MLA_ROPE]
    rb = down[:, 768:768 + MLA_ROPE]
    kpe = ra * cos_ref[...] + rb * sin_ref[...]
    kpe_ref[...] = kpe
    kpeb_ref[...] = kpe.astype(BF16)


def _mla_down(x, w_ext, q_norm, kv_norm, cos, sin, tm=ROW_TILE):
    M, D = x.shape
    tm = math.gcd(M, tm)
    NW = w_ext.shape[1]
    row = lambda n: pl.BlockSpec((tm, n), lambda i: (i, 0))
    full = lambda a, b: pl.BlockSpec((a, b), lambda i: (0, 0))
    return pl.pallas_call(
        _mla_down_kernel,
        grid=(M // tm,),
        in_specs=[row(D), full(D, NW), full(1, MLA_Q_RANK), full(1, MLA_KV_RANK),
                  row(MLA_ROPE), row(MLA_ROPE)],
        out_specs=[row(MLA_Q_RANK), row(MLA_KV_RANK), row(MLA_ROPE), row(MLA_KV_RANK), row(MLA_ROPE)],
        out_shape=[jax.ShapeDtypeStruct((M, MLA_Q_RANK), BF16),
                   jax.ShapeDtypeStruct((M, MLA_KV_RANK), F32),
                   jax.ShapeDtypeStruct((M, MLA_ROPE), F32),
                   jax.ShapeDtypeStruct((M, MLA_KV_RANK), BF16),
                   jax.ShapeDtypeStruct((M, MLA_ROPE), BF16)],
        compiler_params=_cparams("parallel"),
        name="mla_down",
    )(x, w_ext, q_norm.reshape(1, -1), kv_norm.reshape(1, -1), cos, sin)


def _mla_q_kernel(cq_ref, wq_ref, wk_ref, cos_ref, sin_ref, ql_ref, qp_ref):
    cq = cq_ref[...]
    for h in range(wq_ref.shape[0]):
        q = _dot(cq, wq_ref[h])
        nope = q[:, :MLA_NOPE].astype(BF16)
        ra = q[:, MLA_NOPE:MLA_NOPE + MLA_ROPE]
        rb = q[:, MLA_NOPE + LANES:MLA_NOPE + LANES + MLA_ROPE]
        qp_ref[h] = ((ra * cos_ref[...] + rb * sin_ref[...]) * MLA_SCALE).astype(BF16)
        ql_ref[h] = (_dot_nt(nope, wk_ref[h]) * MLA_SCALE).astype(BF16)


def _mla_q(cq, wq_ext, wk, cos, sin, tm=ROW_TILE):
    M = cq.shape[0]
    tm = math.gcd(M, tm)
    H = wq_ext.shape[0]
    hq = math.gcd(H, 4)
    return pl.pallas_call(
        _mla_q_kernel,
        grid=(M // tm, H // hq),
        in_specs=[pl.BlockSpec((tm, MLA_Q_RANK), lambda i, h: (i, 0)),
                  pl.BlockSpec((hq, MLA_Q_RANK, MLA_NOPE + 2 * LANES), lambda i, h: (h, 0, 0)),
                  pl.BlockSpec((hq, MLA_KV_RANK, MLA_NOPE), lambda i, h: (h, 0, 0)),
                  pl.BlockSpec((tm, MLA_ROPE), lambda i, h: (i, 0)),
                  pl.BlockSpec((tm, MLA_ROPE), lambda i, h: (i, 0))],
        out_specs=[pl.BlockSpec((hq, tm, MLA_KV_RANK), lambda i, h: (h, i, 0)),
                   pl.BlockSpec((hq, tm, MLA_ROPE), lambda i, h: (h, i, 0))],
        out_shape=[jax.ShapeDtypeStruct((H, M, MLA_KV_RANK), BF16),
                   jax.ShapeDtypeStruct((H, M, MLA_ROPE), BF16)],
        compiler_params=_cparams("parallel", "parallel"),
        name="mla_q",
    )(cq, wq_ext, wk, cos, sin)


def _online_softmax_step(s, v, m_ref, l_ref, acc_ref):
    m_prev = m_ref[...]
    m_new = jnp.maximum(m_prev, jnp.max(s, -1, keepdims=True))
    alpha = jnp.exp(m_prev - m_new)
    p = jnp.exp(s - m_new)
    l_ref[...] = alpha * l_ref[...] + jnp.sum(p, -1, keepdims=True)
    acc_ref[...] = alpha * acc_ref[...] + _dot(p.astype(BF16), v)
    m_ref[...] = m_new


def _mla_prompt_attn_kernel(ql_ref, qp_ref, ckv_ref, kpe_ref, o_ref, m_ref, l_ref, acc_ref, *, tk):
    H, TQ = ql_ref.shape[0], ql_ref.shape[1]
    q0 = pl.program_id(1) * TQ
    m_ref[...] = jnp.full_like(m_ref, NEG_INF)
    l_ref[...] = jnp.zeros_like(l_ref)
    acc_ref[...] = jnp.zeros_like(acc_ref)
    n_full = q0 // tk
    hp = H // ATTN_HEAD_PARTS
    rp = hp * TQ

    def chunk(kb, masked):
        k0 = pl.multiple_of(kb * tk, tk)
        kc = ckv_ref[pl.ds(k0, tk), :]
        kp = kpe_ref[pl.ds(k0, tk), :]
        if masked:
            qpos = q0 + lax.broadcasted_iota(jnp.int32, (TQ, tk), 0)
            kpos = k0 + lax.broadcasted_iota(jnp.int32, (TQ, tk), 1)
            visible = (kpos <= qpos)[None]
        for part in range(ATTN_HEAD_PARTS):
            heads = slice(part * hp, (part + 1) * hp)
            rows = slice(part * rp, (part + 1) * rp)
            s = (_dot_nt(ql_ref[heads].reshape(rp, MLA_KV_RANK), kc)
                 + _dot_nt(qp_ref[heads].reshape(rp, MLA_ROPE), kp))
            if masked:
                s = jnp.where(visible, s.reshape(hp, TQ, tk), NEG_INF).reshape(rp, tk)
            _online_softmax_step(s, kc, m_ref.at[rows], l_ref.at[rows], acc_ref.at[rows])

    def body(kb, carry):
        chunk(kb, False)
        return carry

    lax.fori_loop(0, n_full, body, 0)
    chunk(n_full, True)
    o_ref[...] = (acc_ref[...] / l_ref[...]).reshape(H, TQ, MLA_KV_RANK).astype(o_ref.dtype)


def _mla_prompt_attn(q_lat, q_pe, ckv_b, kpe_b, B, S, m_total, tk=MLA_CHUNK):
    H = q_lat.shape[0]
    tk = min(tk, S)
    nq = S // Q_BLOCK
    ckv3 = ckv_b[:B * S].reshape(B, S, MLA_KV_RANK)
    kpe3 = kpe_b[:B * S].reshape(B, S, MLA_ROPE)
    return pl.pallas_call(
        functools.partial(_mla_prompt_attn_kernel, tk=tk),
        grid=(B, nq),
        in_specs=[pl.BlockSpec((H, Q_BLOCK, MLA_KV_RANK), lambda b, i: (0, b * nq + i, 0)),
                  pl.BlockSpec((H, Q_BLOCK, MLA_ROPE), lambda b, i: (0, b * nq + i, 0)),
                  pl.BlockSpec((None, S, MLA_KV_RANK), lambda b, i: (b, 0, 0)),
                  pl.BlockSpec((None, S, MLA_ROPE), lambda b, i: (b, 0, 0))],
        out_specs=pl.BlockSpec((H, Q_BLOCK, MLA_KV_RANK), lambda b, i: (0, b * nq + i, 0)),
        out_shape=jax.ShapeDtypeStruct((H, m_total, MLA_KV_RANK), BF16),
        scratch_shapes=[pltpu.VMEM((H * Q_BLOCK, 1), F32), pltpu.VMEM((H * Q_BLOCK, 1), F32),
                        pltpu.VMEM((H * Q_BLOCK, MLA_KV_RANK), F32)],
        compiler_params=_cparams("parallel", "parallel"),
        name="mla_prompt_attn",
    )(q_lat, q_pe, ckv3, kpe3)


def _mla_sample_attn_kernel(pt_ref, ql_ref, qp_ref, cn_ref, kn_ref, *rest, pages, ds):
    ckv_refs = rest[:pages]
    kpe_refs = rest[pages:2 * pages]
    o_ref, m_ref, l_ref, acc_ref = rest[2 * pages:]
    j = pl.program_id(1)

    @pl.when(j == 0)
    def _():
        m_ref[...] = jnp.full_like(m_ref, NEG_INF)
        l_ref[...] = jnp.zeros_like(l_ref)
        acc_ref[...] = jnp.zeros_like(acc_ref)

    ql = ql_ref[...]
    qp = qp_ref[...]
    kcs = [r[...].astype(BF16) for r in ckv_refs]
    s = jnp.concatenate([_dot_nt(ql, kc) + _dot(qp, kr[...].astype(BF16))
                         for kc, kr in zip(kcs, kpe_refs)], axis=1)
    m_prev = m_ref[...]
    m_new = jnp.maximum(m_prev, jnp.max(s, -1, keepdims=True))
    alpha = jnp.exp(m_prev - m_new)
    p = jnp.exp(s - m_new).astype(BF16)
    l_ref[...] = alpha * l_ref[...] + jnp.sum(p.astype(F32), -1, keepdims=True)
    pv = _dot(p[:, :PAGE_SIZE], kcs[0])
    for i in range(1, pages):
        pv += _dot(p[:, i * PAGE_SIZE:(i + 1) * PAGE_SIZE], kcs[i])
    acc_ref[...] = alpha * acc_ref[...] + pv
    m_ref[...] = m_new

    @pl.when(j == pl.num_programs(1) - 1)
    def _():
        cn = cn_ref[...].astype(BF16)
        s_new = _dot_nt(ql, cn) + _dot_nt(qp, kn_ref[...].astype(BF16))
        t = lax.broadcasted_iota(jnp.int32, s_new.shape, 0) % ds
        u = lax.broadcasted_iota(jnp.int32, s_new.shape, 1)
        s_new = jnp.where(u <= t, s_new, NEG_INF)
        _online_softmax_step(s_new, cn, m_ref, l_ref, acc_ref)
        o_ref[...] = (acc_ref[...] / l_ref[...]).astype(o_ref.dtype)


def _mla_sample_attn(q_lat_s, q_pe_s, cnew, knew, cache_ckv, cache_kpe, li, page_table, pages=DECODE_PAGES):
    DB, R, _ = q_lat_s.shape
    ds = cnew.shape[1]
    n_pages = page_table.shape[1]
    pages = min(pages, n_pages)

    def page_spec(rows, width, i):
        return pl.BlockSpec((None, None, rows, width),
                            lambda b, j, pt: (li, pt[b, j * pages + i], 0, 0))

    kpe_t = jnp.transpose(cache_kpe, (0, 1, 3, 2))
    in_specs = [pl.BlockSpec((None, R, MLA_KV_RANK), lambda b, j, pt: (b, 0, 0)),
                pl.BlockSpec((None, R, MLA_ROPE), lambda b, j, pt: (b, 0, 0)),
                pl.BlockSpec((None, ds, MLA_KV_RANK), lambda b, j, pt: (b, 0, 0)),
                pl.BlockSpec((None, ds, MLA_ROPE), lambda b, j, pt: (b, 0, 0))]
    in_specs += [page_spec(PAGE_SIZE, MLA_KV_RANK, i) for i in range(pages)]
    in_specs += [page_spec(MLA_ROPE, PAGE_SIZE, i) for i in range(pages)]
    return pl.pallas_call(
        functools.partial(_mla_sample_attn_kernel, pages=pages, ds=ds),
        grid_spec=pltpu.PrefetchScalarGridSpec(
            num_scalar_prefetch=1,
            grid=(DB, n_pages // pages),
            in_specs=in_specs,
            out_specs=pl.BlockSpec((None, R, MLA_KV_RANK), lambda b, j, pt: (b, 0, 0)),
            scratch_shapes=[pltpu.VMEM((R, 1), F32), pltpu.VMEM((R, 1), F32),
                            pltpu.VMEM((R, MLA_KV_RANK), F32)]),
        out_shape=jax.ShapeDtypeStruct((DB, R, MLA_KV_RANK), BF16),
        compiler_params=_cparams("parallel", "arbitrary"),
        name="mla_sample_attn",
    )(page_table, q_lat_s, q_pe_s, cnew, knew, *([cache_ckv] * pages), *([kpe_t] * pages))


def _mla_out_kernel(ol_ref, wuv_ref, wo_ref, res_ref, g_ref, b_ref, o_ref):
    acc = None
    for h in range(ol_ref.shape[0]):
        oh = _dot(ol_ref[h], wuv_ref[h]).astype(BF16)
        part = _dot(oh, wo_ref[h * MLA_VDIM:(h + 1) * MLA_VDIM, :])
        acc = part if acc is None else acc + part
    o_ref[...] = _layer_norm(ALPHA * res_ref[...] + acc, g_ref[...], b_ref[...])


def _mla_out(o_lat, w_uv, w_o, res, g, b, tm=ROW_TILE):
    H, M, R = o_lat.shape
    tm = math.gcd(M, tm)
    D = w_o.shape[1]
    return pl.pallas_call(
        _mla_out_kernel,
        grid=(M // tm,),
        in_specs=[pl.BlockSpec((H, tm, R), lambda i: (0, i, 0)),
                  pl.BlockSpec((H, R, MLA_VDIM), lambda i: (0, 0, 0)),
                  pl.BlockSpec((H * MLA_VDIM, D), lambda i: (0, 0)),
                  pl.BlockSpec((tm, D), lambda i: (i, 0)),
                  pl.BlockSpec((1, D), lambda i: (0, 0)),
                  pl.BlockSpec((1, D), lambda i: (0, 0))],
        out_specs=pl.BlockSpec((tm, D), lambda i: (i, 0)),
        out_shape=jax.ShapeDtypeStruct((M, D), F32),
        compiler_params=_cparams("parallel"),
        name="mla_out",
    )(o_lat, w_uv, w_o, res, g.reshape(1, D), b.reshape(1, D))


def _mla_layer(X, dims, cos, sin, cache_ckv, cache_kpe, li, page_table,
               w_down, q_norm, kv_norm, w_uq, w_uk, w_uv, w_o, g, b):
    B, S, DB, DS = dims
    MP = B * S
    M, D = X.shape
    H = MLA_HEADS
    rope_w = w_down[:, MLA_Q_RANK + MLA_KV_RANK:]
    zpad = jnp.zeros((D, LANES - MLA_ROPE), F32)
    w_ext = jnp.concatenate([w_down, zpad, _swap_halves(rope_w), zpad], 1).astype(BF16)
    cq, ckv, kpe, ckv_b, kpe_b = _mla_down(X, w_ext, q_norm, kv_norm, cos, sin)
    wq = w_uq.reshape(MLA_Q_RANK, H, MLA_NOPE + MLA_ROPE)
    hpad = jnp.zeros((MLA_Q_RANK, H, LANES - MLA_ROPE), F32)
    wq_ext = jnp.concatenate([wq, hpad, _swap_halves(wq[..., MLA_NOPE:]), hpad], -1)
    wq_ext = jnp.transpose(wq_ext, (1, 0, 2)).astype(BF16)
    wk = jnp.transpose(w_uk, (1, 0, 2)).astype(BF16)
    q_lat, q_pe = _mla_q(cq, wq_ext, wk, cos, sin)
    o_lat = _mla_prompt_attn(q_lat, q_pe, ckv_b, kpe_b, B, S, M)
    to_seq = lambda a: jnp.transpose(a[:, MP:].reshape(H, DB, DS, -1), (1, 0, 2, 3)).reshape(DB, H * DS, -1)
    o_s = _mla_sample_attn(to_seq(q_lat), to_seq(q_pe),
                           ckv[MP:].reshape(DB, DS, -1), kpe[MP:].reshape(DB, DS, -1),
                           cache_ckv, cache_kpe, li, page_table)
    o_s = jnp.transpose(o_s.reshape(DB, H, DS, -1), (1, 0, 2, 3)).reshape(H, DB * DS, -1)
    o_lat = lax.dynamic_update_slice(o_lat, o_s, (0, MP, 0))
    wuv = jnp.transpose(w_uv, (1, 0, 2)).astype(BF16)
    X = _mla_out(o_lat, wuv, w_o.astype(BF16), X, g, b)
    return X, ckv, kpe


def _np_t5_bucket(rel):
    n = np.maximum(rel, 0)
    exact = N_BUCKETS // 2
    large = exact + (np.log(np.maximum(n, 1) / exact) / math.log(MAX_DISTANCE / exact)
                     * (N_BUCKETS - exact)).astype(np.int64)
    return np.where(n < exact, n, np.minimum(large, N_BUCKETS - 1)).astype(np.int32)


def _bias_tile_kernel(tab_ref, bkt_ref, o_ref):
    bkt = bkt_ref[...]
    for h in range(NSA_HEADS):
        far = tab_ref[N_BUCKETS - 1, h]
        acc = jnp.zeros(bkt.shape, F32)
        for b in range(N_BUCKETS - 1):
            acc = jnp.where(bkt == b, tab_ref[b, h] - far, acc)
        o_ref[h] = acc


def _bias_tiles(table, buckets):
    R, C = buckets.shape
    return pl.pallas_call(
        _bias_tile_kernel,
        in_specs=[pl.BlockSpec(memory_space=pltpu.SMEM), pl.BlockSpec(memory_space=pltpu.VMEM)],
        out_specs=pl.BlockSpec(memory_space=pltpu.VMEM),
        out_shape=jax.ShapeDtypeStruct((NSA_HEADS, R, C), F32),
        name="nsa_bias_tiles",
    )(table, jnp.asarray(buckets))


def _hi_lo(a):
    hi = a.astype(BF16)
    lo = (a - hi.astype(F32)).astype(BF16)
    return hi, lo


def _nsa_in_kernel(x_ref, w_ref, q_ref, kv_ref, kvb_ref, gate_ref):
    h = _dot(x_ref[...].astype(BF16), w_ref[...])
    qw = NSA_HEADS * LANES
    for hd in range(NSA_HEADS):
        q_ref[hd] = (h[:, hd * LANES:(hd + 1) * LANES] * NSA_SCALE).astype(BF16)
    for j in range(6):
        piece = h[:, qw + j * LANES:qw + (j + 1) * LANES]
        kv_ref[j] = piece
        kvb_ref[j] = piece.astype(BF16)
    gate_ref[...] = jax.nn.sigmoid(h[:, qw + 6 * LANES:])


def _nsa_in(x, w_ext, tm=ROW_TILE):
    M, D = x.shape
    tm = math.gcd(M, tm)
    NW = w_ext.shape[1]
    return pl.pallas_call(
        _nsa_in_kernel,
        grid=(M // tm,),
        in_specs=[pl.BlockSpec((tm, D), lambda i: (i, 0)),
                  pl.BlockSpec((D, NW), lambda i: (0, 0))],
        out_specs=[pl.BlockSpec((NSA_HEADS, tm, LANES), lambda i: (0, i, 0)),
                   pl.BlockSpec((6, tm, LANES), lambda i: (0, i, 0)),
                   pl.BlockSpec((6, tm, LANES), lambda i: (0, i, 0)),
                   pl.BlockSpec((tm, LANES), lambda i: (i, 0))],
        out_shape=[jax.ShapeDtypeStruct((NSA_HEADS, M, LANES), BF16),
                   jax.ShapeDtypeStruct((6, M, LANES), F32),
                   jax.ShapeDtypeStruct((6, M, LANES), BF16),
                   jax.ShapeDtypeStruct((M, LANES), F32)],
        compiler_params=_cparams("parallel"),
        name="nsa_in",
    )(x, w_ext)


def _pad_heads_to_group_lanes(w, axis):
    z = jnp.zeros_like(w)
    lo = jnp.concatenate([w, z], axis + 1)
    hi = jnp.concatenate([z, w], axis + 1)
    shape = [1] * w.ndim
    shape[axis] = NSA_HEADS
    first = (jnp.arange(NSA_HEADS) < NSA_HPG).reshape(shape)
    return jnp.where(first, lo, hi)


def _compress_weights(pe, w1, w2):
    eye = jnp.eye(NSA_GROUPS, dtype=F32)
    half = lambda w: jnp.einsum('rde,gh->rgdhe', w, eye).reshape(CMP_STRIDE * LANES, LANES)
    w_big = jnp.concatenate([half(w1[:CMP_STRIDE]), half(w1[CMP_STRIDE:])], 1)
    flat = lambda p: jnp.tile(p[:, None, :], (1, NSA_GROUPS, 1)).reshape(1, CMP_STRIDE * LANES)
    pe2 = jnp.concatenate([flat(pe[:CMP_STRIDE]), flat(pe[CMP_STRIDE:]),
                           jnp.zeros((6, CMP_STRIDE * LANES), F32)], 0)
    return w_big.astype(BF16), pe2.astype(BF16), jnp.kron(eye, w2).astype(BF16)


def _compress_blocks(fs, const, w2):
    nh = fs.shape[0]
    nxt = pltpu.roll(fs[:, LANES:], nh - 1, 0)
    pre = fs[:, :LANES] + nxt + const
    out = _dot((pre * jax.nn.sigmoid(pre)).astype(BF16), w2)
    row = lax.broadcasted_iota(jnp.int32, out.shape, 0)
    return jnp.where(row < nh - 1, out, 0.0).astype(BF16)


def _nsa_compress_kernel(x_ref, w_ref, pe_ref, w2_ref, o_ref, c_ref):
    fs = _dot(x_ref[...], w_ref[...])
    cst = _dot(pe_ref[...], w_ref[...])
    const = cst[0:1, :LANES] + cst[1:2, LANES:]
    c_ref[...] = jnp.broadcast_to(const, c_ref.shape)
    o_ref[...] = _compress_blocks(fs, const, w2_ref[...])


def _nsa_compress(x, w_big, pe2, w2):
    _, B, nh, K = x.shape
    return pl.pallas_call(
        _nsa_compress_kernel,
        grid=(2, B),
        in_specs=[pl.BlockSpec((None, None, nh, K), lambda t, b: (t, b, 0, 0)),
                  pl.BlockSpec((None, K, 2 * LANES), lambda t, b: (t, 0, 0)),
                  pl.BlockSpec((None, 8, K), lambda t, b: (t, 0, 0)),
                  pl.BlockSpec((None, LANES, LANES), lambda t, b: (t, 0, 0))],
        out_specs=[pl.BlockSpec((None, None, nh, LANES), lambda t, b: (t, b, 0, 0)),
                   pl.BlockSpec((None, None, 8, LANES), lambda t, b: (t, b, 0, 0))],
        out_shape=[jax.ShapeDtypeStruct((2, B, nh, LANES), BF16),
                   jax.ShapeDtypeStruct((2, B, 8, LANES), F32)],
        compiler_params=_cparams("parallel", "parallel"),
        name="nsa_compress",
    )(x, w_big, pe2, w2)


def _top_blocks(work, blk, n_pick):
    ns = work.shape[-1]
    sel = jnp.zeros(work.shape, F32)
    for _ in range(n_pick):
        best = jnp.max(work, -1, keepdims=True)
        first = jnp.min(jnp.where(work == best, blk, ns), -1, keepdims=True)
        hit = blk == first
        sel = jnp.where(hit, 1.0, sel)
        work = jnp.where(hit, NEG_INF, work)
    return sel


def _block_importance(p_grp, ns):
    nh = p_grp.shape[-1]
    cc = lax.broadcasted_iota(jnp.int32, (nh, ns), 0)
    jj = lax.broadcasted_iota(jnp.int32, (nh, ns), 1) * (SEL_BLOCK // CMP_STRIDE)
    cover = jnp.where((cc >= jj - 1) & (cc <= jj + 3), 1.0, 0.0).astype(BF16)
    hi = p_grp.astype(BF16)
    r1 = p_grp - hi.astype(F32)
    mid = r1.astype(BF16)
    lo = (r1 - mid.astype(F32)).astype(BF16)
    return _dot(hi, cover) + _dot(mid, cover) + _dot(lo, cover)


def _nsa_prompt_kernel(q_ref, kc_ref, vc_ref, ks_ref, vs_ref, kw_ref, vw_ref, gate_ref, akey_ref, aband_ref,
                       o_ref, m_ref, l_ref, acc_ref, *, tk):
    H, TQ = q_ref.shape[0], q_ref.shape[1]
    R = H * TQ
    G = NSA_GROUPS
    S = ks_ref.shape[0]
    nh = kc_ref.shape[0]
    ns = S // SEL_BLOCK
    qb = pl.program_id(1)
    q0 = qb * TQ
    q = q_ref[...].reshape(R, LANES)
    akey = akey_ref[...].reshape(R, 2 * BAND)
    i32 = jnp.int32

    def key_band(k_first, n_keys):
        d = k_first - q0 + Q_BLOCK + lax.broadcasted_iota(i32, (n_keys, 2 * BAND), 0)
        slot = lax.broadcasted_iota(i32, (n_keys, 2 * BAND), 1) & (BAND - 1)
        return _dot_nt(akey, jnp.where(d == slot, 1.0, 0.0).astype(BF16))

    c_oh = lax.broadcasted_iota(i32, (nh, LANES), 0) - (qb * (Q_BLOCK // CMP_STRIDE) - CMP_BAND_LEAD)
    m_oh = lax.broadcasted_iota(i32, (nh, LANES), 1)
    oh = jnp.where((c_oh == (m_oh & (CMP_BAND - 1))) & (m_oh < 2 * CMP_BAND), 1.0, 0.0).astype(BF16)
    s_c = _dot_nt(q, kc_ref[...]) + _dot_nt(aband_ref[...].reshape(R, LANES), oh)
    c_end = lax.broadcasted_iota(i32, (TQ, nh), 1) * CMP_STRIDE + (CMP_LEN - 1)
    q_pos = q0 + lax.broadcasted_iota(i32, (TQ, nh), 0)
    s_c = s_c.reshape(H, TQ, nh) + jnp.where(c_end <= q_pos, 0.0, NEG_INF)[None]
    m_c = jnp.max(s_c, -1, keepdims=True)
    m_c = jnp.where(m_c == NEG_INF, 0.0, m_c)
    e_c = jnp.exp(s_c - m_c)
    d_c = jnp.sum(e_c, -1, keepdims=True)
    p_c = e_c / jnp.where(d_c > 0, d_c, 1.0)
    o_c = _dot(p_c.reshape(R, nh).astype(BF16), vc_ref[...])

    p_grp = jnp.sum(p_c.reshape(G, NSA_HPG, TQ, nh), axis=1).reshape(G * TQ, nh)
    imp = _block_importance(p_grp, ns)
    blk = lax.broadcasted_iota(i32, (G * TQ, ns), 1)
    cur = (q0 + (lax.broadcasted_iota(i32, (G * TQ, ns), 0) & (TQ - 1))) >> SEL_SHIFT
    forced = (blk == 0) | (blk == cur) | (blk == cur - 1)
    work = jnp.where(forced, jnp.inf, jnp.where(blk <= cur, imp, NEG_INF))
    sel_b = _top_blocks(work, blk, min(N_SEL, ns)).astype(BF16)

    m_ref[...] = jnp.full_like(m_ref, NEG_BIG)
    l_ref[...] = jnp.zeros_like(l_ref)
    acc_ref[...] = jnp.zeros_like(acc_ref)

    def sel_chunk(kb, near):
        k0 = pl.multiple_of(kb * tk, tk)
        s = _dot_nt(q, ks_ref[pl.ds(k0, tk), :])
        eb = lax.broadcasted_iota(i32, (ns, tk), 0)
        ek = (k0 + lax.broadcasted_iota(i32, (ns, tk), 1)) >> SEL_SHIFT
        allow = _dot(sel_b, jnp.where(eb == ek, 1.0, 0.0).astype(BF16))
        mask = (allow - 1.0) * (-NEG_BIG)
        if near:
            s = s + key_band(k0, tk)
            qpos = q0 + (lax.broadcasted_iota(i32, (G * TQ, tk), 0) & (TQ - 1))
            kpos = k0 + lax.broadcasted_iota(i32, (G * TQ, tk), 1)
            mask = jnp.where(kpos <= qpos, mask, NEG_BIG)
        s = (s.reshape(G, NSA_HPG, TQ, tk) + mask.reshape(G, 1, TQ, tk)).reshape(R, tk)
        _online_softmax_step(s, vs_ref[pl.ds(k0, tk), :], m_ref, l_ref, acc_ref)

    n_last = q0 // tk
    n_far = jnp.maximum(q0 - Q_BLOCK, 0) // tk

    def far_body(kb, carry):
        sel_chunk(kb, False)
        return carry

    lax.fori_loop(0, n_far, far_body, 0)

    @pl.when(n_far < n_last)
    def _():
        sel_chunk(n_far, True)

    sel_chunk(n_last, True)
    o_s = acc_ref[...] / l_ref[...]

    nw = WINDOW + TQ
    w0 = pl.multiple_of(jnp.maximum(q0 - WINDOW, 0), TQ)
    s_w = _dot_nt(q, kw_ref[pl.ds(w0, nw), :]) + key_band(w0, nw)
    rel = q0 + lax.broadcasted_iota(i32, (TQ, nw), 0) - (w0 + lax.broadcasted_iota(i32, (TQ, nw), 1))
    s_w = s_w.reshape(H, TQ, nw) + jnp.where((rel >= 0) & (rel < WINDOW), 0.0, NEG_INF)[None]
    e_w = jnp.exp(s_w - jnp.max(s_w, -1, keepdims=True))
    d_w = jnp.sum(e_w, -1, keepdims=True).reshape(R, 1)
    o_w = _dot(e_w.reshape(R, nw).astype(BF16), vw_ref[pl.ds(w0, nw), :]) / d_w

    gates = gate_ref[...]
    for h in range(H):
        rows = slice(h * TQ, (h + 1) * TQ)
        o_h = (gates[:, h:h + 1] * o_c[rows] + gates[:, H + h:H + h + 1] * o_s[rows]
               + gates[:, 2 * H + h:2 * H + h + 1] * o_w[rows])
        o_ref[:, h * LANES:(h + 1) * LANES] = o_h.astype(o_ref.dtype)


def _nsa_prompt_attn(q, cmp_kv, kvb, gates, akey, aband, B, S, m_total):
    H = q.shape[0]
    nq = S // Q_BLOCK
    nh = cmp_kv.shape[2]
    tk = min(SEL_CHUNK, S)
    kv3 = kvb[:, :B * S].reshape(6, B, S, LANES)

    def stream(j):
        return pl.BlockSpec((None, None, S, LANES), lambda b, i: (j, b, 0, 0))

    def cmp(j):
        return pl.BlockSpec((None, None, nh, LANES), lambda b, i: (j, b, 0, 0))

    return pl.pallas_call(
        functools.partial(_nsa_prompt_kernel, tk=tk),
        grid=(B, nq),
        in_specs=[pl.BlockSpec((H, Q_BLOCK, LANES), lambda b, i: (0, b * nq + i, 0)),
                  cmp(0), cmp(1), stream(2), stream(3), stream(4), stream(5),
                  pl.BlockSpec((Q_BLOCK, LANES), lambda b, i: (b * nq + i, 0)),
                  pl.BlockSpec((H, Q_BLOCK, 2 * BAND), lambda b, i: (0, 0, 0)),
                  pl.BlockSpec((H, Q_BLOCK, LANES), lambda b, i: (0, 0, 0))],
        out_specs=pl.BlockSpec((Q_BLOCK, H * LANES), lambda b, i: (b * nq + i, 0)),
        out_shape=jax.ShapeDtypeStruct((m_total, H * LANES), BF16),
        scratch_shapes=[pltpu.VMEM((H * Q_BLOCK, 1), F32), pltpu.VMEM((H * Q_BLOCK, 1), F32),
                        pltpu.VMEM((H * Q_BLOCK, LANES), F32)],
        compiler_params=_cparams("parallel", "parallel"),
        name="nsa_prompt_attn",
    )(q, cmp_kv, cmp_kv, kv3, kv3, kv3, kv3, gates, akey, aband)


def _feature_major_pages(cache):
    L, POOL = cache.shape[:2]
    return jnp.transpose(cache, (0, 1, 3, 4, 2)).reshape(L, POOL, LANES, cache.shape[2])


def _nsa_fs_kernel(pt_ref, w_ref, *rest):
    page_refs, (o_ref, rows_ref) = rest[:-2], rest[-2:]
    for i, r in enumerate(page_refs):
        rows_ref[i * PAGE_SIZE:(i + 1) * PAGE_SIZE, :] = r[...].T
    n_half = o_ref.shape[0]
    acc = None
    for pos in range(CMP_STRIDE):
        x = rows_ref[pl.ds(pos, n_half, stride=CMP_STRIDE), :].astype(BF16)
        part = _dot(x, w_ref[pos])
        acc = part if acc is None else acc + part
    o_ref[...] = acc


def _nsa_fs_sample(cache, li, page_table, w_big, pages=16):
    DB, n_pages = page_table.shape
    pages = min(pages, n_pages)
    hb = PAGE_SIZE // CMP_STRIDE
    view = _feature_major_pages(cache)
    w_pos = w_big.reshape(CMP_STRIDE, LANES, 2 * LANES)

    def page_spec(i):
        return pl.BlockSpec((None, None, LANES, PAGE_SIZE), lambda b, j, pt: (li, pt[b, j * pages + i], 0, 0))

    return pl.pallas_call(
        _nsa_fs_kernel,
        grid_spec=pltpu.PrefetchScalarGridSpec(
            num_scalar_prefetch=1,
            grid=(DB, n_pages // pages),
            in_specs=[pl.BlockSpec((CMP_STRIDE, LANES, 2 * LANES), lambda b, j, pt: (0, 0, 0))]
                     + [page_spec(i) for i in range(pages)],
            out_specs=pl.BlockSpec((None, pages * hb, 2 * LANES), lambda b, j, pt: (b, j, 0)),
            scratch_shapes=[pltpu.VMEM((pages * PAGE_SIZE, LANES), F32)]),
        out_shape=jax.ShapeDtypeStruct((DB, n_pages * hb, 2 * LANES), F32),
        compiler_params=_cparams("parallel", "parallel"),
        name="nsa_fs_sample",
    )(page_table, w_pos, *([view] * pages))


def _nsa_sample_kernel(pt_ref, q_ref, fsk_ref, fsv_ref, const_ref, w2_ref, skw_ref, svw_ref, new_ref, gate_ref,
                       tile_ref, *rest, pages, ds, past):
    ks_refs = rest[:pages]
    vs_refs = rest[pages:2 * pages]
    o_ref, sel_ref, oc_ref, m_ref, l_ref, acc_ref = rest[2 * pages:]
    j = pl.program_id(1)
    n_steps = pl.num_programs(1)
    H, G = NSA_HEADS, NSA_GROUPS
    R = ds * H
    nh = fsk_ref.shape[0]
    ns = past // SEL_BLOCK
    win = skw_ref.shape[1]
    i32 = jnp.int32
    q = q_ref[...]
    tok = lax.broadcasted_iota(i32, (R, 1), 0) >> HEAD_SHIFT

    @pl.when(j == 0)
    def _():
        kcb = _compress_blocks(fsk_ref[...], const_ref[0], w2_ref[0])
        vcb = _compress_blocks(fsv_ref[...], const_ref[1], w2_ref[1])
        s_c = _dot_nt(q, kcb) + tile_ref[:, TILE_CMP:TILE_CMP + nh]
        c_end = lax.broadcasted_iota(i32, (R, nh), 1) * CMP_STRIDE + (CMP_LEN - 1)
        s_c = jnp.where(c_end <= past + tok, s_c, NEG_INF)
        m_c = jnp.max(s_c, -1, keepdims=True)
        m_c = jnp.where(m_c == NEG_INF, 0.0, m_c)
        e_c = jnp.exp(s_c - m_c)
        d_c = jnp.sum(e_c, -1, keepdims=True)
        p_c = e_c / jnp.where(d_c > 0, d_c, 1.0)
        oc_ref[...] = _dot(p_c.astype(BF16), vcb)
        p_grp = jnp.sum(p_c.reshape(ds * G, NSA_HPG, nh), axis=1)
        imp = _block_importance(p_grp, ns)
        blk = lax.broadcasted_iota(i32, (ds * G, ns), 1)
        work = jnp.where((blk == 0) | (blk == ns - 1), jnp.inf, imp)
        sel_ref[...] = _top_blocks(work, blk, min(N_SEL - 1, ns))
        m_ref[...] = jnp.full_like(m_ref, NEG_BIG)
        l_ref[...] = jnp.zeros_like(l_ref)
        acc_ref[...] = jnp.zeros_like(acc_ref)

    def online(s, v_list, feature_major):
        pv_dot = _dot_nt if feature_major else _dot
        m_prev = m_ref[...]
        m_new = jnp.maximum(m_prev, jnp.max(s, -1, keepdims=True))
        alpha = jnp.exp(m_prev - m_new)
        p = jnp.exp(s - m_new)
        l_ref[...] = alpha * l_ref[...] + jnp.sum(p, -1, keepdims=True)
        pb = p.astype(BF16)
        width = s.shape[1] // len(v_list)
        pv = pv_dot(pb[:, :width], v_list[0])
        for i in range(1, len(v_list)):
            pv += pv_dot(pb[:, i * width:(i + 1) * width], v_list[i])
        acc_ref[...] = alpha * acc_ref[...] + pv
        m_ref[...] = m_new

    nk = pages * PAGE_SIZE
    is_last = (j == n_steps - 1).astype(F32)
    s_parts = []
    for i in range(pages):
        s_i = _dot(q, ks_refs[i][...].astype(BF16))
        if i == pages - 1:
            s_i = s_i + is_last * tile_ref[:, TILE_LAST:TILE_LAST + PAGE_SIZE]
        s_parts.append(s_i)
    s = jnp.concatenate(s_parts, axis=1)
    eb = lax.broadcasted_iota(i32, (ns, nk), 0)
    ek = (j * nk + lax.broadcasted_iota(i32, (ns, nk), 1)) >> SEL_SHIFT
    allow = _dot(sel_ref[...].astype(BF16), jnp.where(eb == ek, 1.0, 0.0).astype(BF16))
    mask = (allow - 1.0) * (-NEG_BIG)
    s = (s.reshape(ds * G, NSA_HPG, nk) + mask[:, None, :]).reshape(R, nk)
    online(s, [r[...].astype(BF16) for r in vs_refs], True)

    @pl.when(j == n_steps - 1)
    def _():
        new_b = new_ref[...].astype(BF16)
        u = lax.broadcasted_iota(i32, (R, ds), 1)
        nb = tile_ref[:, TILE_NEW:TILE_NEW + ds] + jnp.where(u <= tok, 0.0, NEG_INF)
        online(_dot_nt(q, new_b[0]) + nb, [new_b[1]], False)
        o_s = acc_ref[...] / l_ref[...]
        idx = lax.broadcasted_iota(i32, (R, win), 1)
        s1 = _dot(q, skw_ref[...].astype(BF16)) + tile_ref[:, TILE_WIN:TILE_WIN + win]
        s1 = jnp.where(idx > tok + (win - WINDOW), s1, NEG_INF)
        s2 = _dot_nt(q, new_b[2]) + nb
        m_w = jnp.maximum(jnp.max(s1, -1, keepdims=True), jnp.max(s2, -1, keepdims=True))
        e1 = jnp.exp(s1 - m_w)
        e2 = jnp.exp(s2 - m_w)
        d_w = jnp.sum(e1, -1, keepdims=True) + jnp.sum(e2, -1, keepdims=True)
        o_w = (_dot_nt(e1.astype(BF16), svw_ref[...].astype(BF16)) + _dot(e2.astype(BF16), new_b[3])) / d_w
        g = gate_ref[...]
        o_ref[...] = (g[0] * oc_ref[...] + g[1] * o_s + g[2] * o_w).astype(o_ref.dtype)


def _nsa_sample_attn(q_s, fs_k, fs_v, const, w2, state_kw, state_vw, new4, gates_s, tiles,
                     cache_ks, cache_vs, li, page_table, pages=DECODE_PAGES):
    DB, R, _ = q_s.shape
    ds = new4.shape[2]
    n_pages = page_table.shape[1]
    pages = min(pages, n_pages)
    past = n_pages * PAGE_SIZE
    nh = fs_k.shape[1]
    win = state_kw.shape[2]
    L, POOL = cache_ks.shape[:2]
    ks_view = _feature_major_pages(cache_ks)
    vs_view = _feature_major_pages(cache_vs)
    skw = _feature_major_pages(state_kw)
    svw = _feature_major_pages(state_vw)

    def page_spec(i):
        return pl.BlockSpec((None, None, LANES, PAGE_SIZE), lambda b, j, pt: (li, pt[b, j * pages + i], 0, 0))

    per_seq = lambda *shape: pl.BlockSpec((None,) + shape, lambda b, j, pt: (b,) + (0,) * len(shape))
    in_specs = [per_seq(R, LANES), per_seq(nh, 2 * LANES), per_seq(nh, 2 * LANES),
                pl.BlockSpec((2, 1, LANES), lambda b, j, pt: (0, 0, 0)),
                pl.BlockSpec((2, LANES, LANES), lambda b, j, pt: (0, 0, 0)),
                pl.BlockSpec((None, None, LANES, win), lambda b, j, pt: (li, b, 0, 0)),
                pl.BlockSpec((None, None, LANES, win), lambda b, j, pt: (li, b, 0, 0)),
                per_seq(4, ds, LANES), per_seq(3, R, 1),
                pl.BlockSpec(tiles.shape, lambda b, j, pt: (0, 0))]
    in_specs += [page_spec(i) for i in range(pages)] * 2
    return pl.pallas_call(
        functools.partial(_nsa_sample_kernel, pages=pages, ds=ds, past=past),
        grid_spec=pltpu.PrefetchScalarGridSpec(
            num_scalar_prefetch=1,
            grid=(DB, n_pages // pages),
            in_specs=in_specs,
            out_specs=per_seq(R, LANES),
            scratch_shapes=[pltpu.VMEM((ds * NSA_GROUPS, past // SEL_BLOCK), F32),
                            pltpu.VMEM((R, LANES), F32),
                            pltpu.VMEM((R, 1), F32), pltpu.VMEM((R, 1), F32), pltpu.VMEM((R, LANES), F32)]),
        out_shape=jax.ShapeDtypeStruct((DB, R, LANES), BF16),
        compiler_params=_cparams("parallel", "arbitrary"),
        name="nsa_sample_attn",
    )(page_table, q_s, fs_k, fs_v, const, w2, skw, svw, new4, gates_s, tiles,
      *([ks_view] * pages), *([vs_view] * pages))


def _nsa_bias_inputs(table, ds, past, win):
    i = np.arange(Q_BLOCK)[:, None]
    rel_key = i - np.arange(BAND)[None, :] + Q_BLOCK
    rel_cmp = i - CMP_STRIDE * (np.arange(CMP_BAND)[None, :] - CMP_BAND_LEAD) - (CMP_LEN - 1)
    sheet_p = np.concatenate([rel_key, rel_cmp, np.zeros((Q_BLOCK, LANES - CMP_BAND), np.int64)], 1)
    tp = _bias_tiles(table, _np_t5_bucket(sheet_p))
    hi, lo = _hi_lo(tp[:, :, :BAND])
    akey = jnp.concatenate([hi, lo], -1)
    hi, lo = _hi_lo(tp[:, :, BAND:BAND + CMP_BAND])
    aband = jnp.concatenate([hi, lo, jnp.zeros((NSA_HEADS, Q_BLOCK, LANES - 2 * CMP_BAND), BF16)], -1)
    t = np.arange(ds)[:, None]
    nh = past // CMP_STRIDE
    assert nh <= TILE_WIN - TILE_CMP and win <= TILE_LAST - TILE_WIN and ds <= SEL_BLOCK
    pad = lambda a, w: np.concatenate([a, np.zeros((ds, w - a.shape[1]), np.int64)], 1)
    rel_c = past + t - CMP_STRIDE * np.arange(nh)[None, :] - (CMP_LEN - 1)
    rel_w = t + win - np.arange(win)[None, :]
    rel_l = t + PAGE_SIZE - np.arange(PAGE_SIZE)[None, :]
    rel_n = t - np.arange(ds)[None, :]
    sheet_s = np.concatenate([pad(rel_c, TILE_WIN - TILE_CMP), pad(rel_w, TILE_LAST - TILE_WIN),
                              pad(rel_l, TILE_NEW - TILE_LAST), pad(rel_n, LANES)], 1)
    ts = _bias_tiles(table, _np_t5_bucket(sheet_s))
    tiles = jnp.transpose(ts, (1, 0, 2)).reshape(ds * NSA_HEADS, -1)
    return akey, aband, tiles


def _nsa_layer(X, dims, li, bias_inputs, cache_kc, cache_vc, cache_ks, cache_vs, state_kw, state_vw, page_table,
               w_in, pe_k, w1_k, w2_k, pe_v, w1_v, w2_v, w_o, g, b):
    B, S, DB, DS = dims
    MP = B * S
    M, D = X.shape
    H = NSA_HEADS
    akey, aband, tiles = bias_inputs
    qw = H * NSA_DH
    kvw = 6 * NSA_GROUPS * NSA_DH
    wq = _pad_heads_to_group_lanes(w_in[:, :qw].reshape(D, H, NSA_DH), 1).reshape(D, H * LANES)
    w_ext = jnp.concatenate([wq, w_in[:, qw:], jnp.zeros((D, LANES - (w_in.shape[1] - qw - kvw)), F32)], 1)
    q, kv, kvb, gates = _nsa_in(X, w_ext.astype(BF16))
    wk_big, pe2_k, w2k = _compress_weights(pe_k, w1_k, w2_k)
    wv_big, pe2_v, w2v = _compress_weights(pe_v, w1_v, w2_v)
    w2 = jnp.stack([w2k, w2v])
    nh = S // CMP_STRIDE
    cmp_kv, const = _nsa_compress(kvb[:2, :MP].reshape(2, B, nh, CMP_STRIDE * LANES),
                                  jnp.stack([wk_big, wv_big]), jnp.stack([pe2_k, pe2_v]), w2)
    o = _nsa_prompt_attn(q, cmp_kv, kvb, gates, akey, aband, B, S, M)
    q_s = jnp.transpose(q[:, MP:].reshape(H, DB, DS, LANES), (1, 2, 0, 3)).reshape(DB, DS * H, LANES)
    fs_k = _nsa_fs_sample(cache_kc, li, page_table, wk_big)
    fs_v = _nsa_fs_sample(cache_vc, li, page_table, wv_big)
    new4 = jnp.transpose(kv[2:, MP:].reshape(4, DB, DS, LANES), (1, 0, 2, 3))
    gates_s = jnp.transpose(gates[MP:, :3 * H].reshape(DB, DS, 3, H), (0, 2, 1, 3)).reshape(DB, 3, DS * H, 1)
    o_s = _nsa_sample_attn(q_s, fs_k, fs_v, const[:, 0, :1], w2, state_kw, state_vw, new4, gates_s, tiles,
                           cache_ks, cache_vs, li, page_table)
    o = lax.dynamic_update_slice(o, o_s.reshape(DB * DS, H * LANES), (MP, 0))
    wo = _pad_heads_to_group_lanes(w_o.reshape(H, NSA_DH, D), 0).reshape(H * LANES, D)
    X = _mm_ln(o, wo.astype(BF16), X, g, b, name="nsa_out")
    split = lambda a: (a[:MP].reshape(B, S, NSA_GROUPS, NSA_DH), a[MP:].reshape(DB, DS, NSA_GROUPS, NSA_DH))
    pieces = [split(kv[j]) for j in range(6)]
    keep = min(WINDOW, S)
    outs_p = [p for p, _ in pieces[:4]] + [pieces[4][0][:, S - keep:], pieces[5][0][:, S - keep:]]
    outs_s = [s for _, s in pieces[:4]] + [jnp.concatenate([state_kw[li], pieces[4][1]], 1)[:, DS:],
                                           jnp.concatenate([state_vw[li], pieces[5][1]], 1)[:, DS:]]
    return X, outs_p, outs_s


def kernel(x_prompt, x_sample, cache_mla_ckv, cache_mla_kpe, cache_nsa_kc, cache_nsa_vc, cache_nsa_ks, cache_nsa_vs, state_nsa_kw, state_nsa_vw, cache_mem_k, cache_mem_v, page_table, mem_prompt, ln_g, ln_b, mla_w_down, mla_q_norm, mla_kv_norm, mla_w_uq, mla_w_uk, mla_w_uv, mla_w_o, nsa_w_in, nsa_cmp_pe_k, nsa_cmp_w1_k, nsa_cmp_w2_k, nsa_cmp_pe_v, nsa_cmp_w1_v, nsa_cmp_w2_v, nsa_w_o, rel_bias, xa_w_q, xa_w_kv, xa_w_o, ffn_w_gu, ffn_w_down, moe_router, moe_w_gu, moe_w_down):
    B, S, D = x_prompt.shape
    DB, DS, _ = x_sample.shape
    dims = (B, S, DB, DS)
    MP, MS = B * S, DB * DS
    past = page_table.shape[1] * PAGE_SIZE
    n_mem = mem_prompt.shape[1]
    depth = ln_g.shape[0]
    HD = MEM_HEADS * MEM_DH

    X = jnp.concatenate([x_prompt.reshape(MP, D), x_sample.reshape(MS, D)], 0)
    pos = jnp.concatenate([jnp.tile(jnp.arange(S, dtype=jnp.int32), B),
                           jnp.tile(past + jnp.arange(DS, dtype=jnp.int32), DB)])
    cos, sin = _rope_tables(pos)
    mem_rows = mem_prompt.reshape(B * n_mem, D)
    bias_inputs = _nsa_bias_inputs(rel_bias, DS, past, state_nsa_kw.shape[2])

    outs_p = {k: [] for k in ("ckv", "kpe", "kc", "vc", "ks", "vs", "kw", "vw", "mk", "mv")}
    outs_s = {k: [] for k in ("ckv", "kpe", "kc", "vc", "ks", "vs", "kw", "vw")}
    for i in range(depth):
        li = i // 2
        if i % 2 == 0:
            X, ckv, kpe = _mla_layer(X, dims, cos, sin, cache_mla_ckv, cache_mla_kpe, li, page_table,
                                     mla_w_down[li], mla_q_norm[li], mla_kv_norm[li], mla_w_uq[li],
                                     mla_w_uk[li], mla_w_uv[li], mla_w_o[li], ln_g[i, 0], ln_b[i, 0])
            outs_p["ckv"].append(ckv[:MP].reshape(B, S, -1))
            outs_p["kpe"].append(kpe[:MP].reshape(B, S, -1))
            outs_s["ckv"].append(ckv[MP:].reshape(DB, DS, -1))
            outs_s["kpe"].append(kpe[MP:].reshape(DB, DS, -1))
        else:
            X, o_p, o_s = _nsa_layer(X, dims, li, bias_inputs, cache_nsa_kc, cache_nsa_vc, cache_nsa_ks,
                                     cache_nsa_vs, state_nsa_kw, state_nsa_vw, page_table,
                                     nsa_w_in[li], nsa_cmp_pe_k[li], nsa_cmp_w1_k[li], nsa_cmp_w2_k[li],
                                     nsa_cmp_pe_v[li], nsa_cmp_w1_v[li], nsa_cmp_w2_v[li], nsa_w_o[li],
                                     ln_g[i, 0], ln_b[i, 0])
            for k, v in zip(("kc", "vc", "ks", "vs", "kw", "vw"), o_p):
                outs_p[k].append(v)
            for k, v in zip(("kc", "vc", "ks", "vs", "kw", "vw"), o_s):
                outs_s[k].append(v)
        kv = _mm(mem_rows, xa_w_kv[i].astype(BF16), name="mem_kv").reshape(B, n_mem, 2, MEM_HEADS, MEM_DH)
        mk, mv = kv[:, :, 0], kv[:, :, 1]
        outs_p["mk"].append(mk)
        outs_p["mv"].append(mv)
        q = _mm(X, xa_w_q[i].astype(BF16), out_dtype=BF16, scale=MEM_SCALE, name="xattn_q")
        o_p = _xattn(q[:MP].reshape(B, S, HD), mk.reshape(B, n_mem, HD), mv.reshape(B, n_mem, HD),
                     rows=min(ROW_TILE, S), group=1)
        o_s = _xattn(q[MP:].reshape(DB, DS, HD), cache_mem_k[i].reshape(DB, n_mem, HD),
                     cache_mem_v[i].reshape(DB, n_mem, HD), rows=DS, group=min(8, DB))
        o = jnp.concatenate([o_p.reshape(MP, HD), o_s.reshape(MS, HD)], 0)
        X = _mm_ln(o, xa_w_o[i].astype(BF16), X, ln_g[i, 1], ln_b[i, 1], name="xattn_out")
        if i % 2 == 0:
            X = _ffn_ln(X, ffn_w_gu[li][None].astype(BF16), ffn_w_down[li][None].astype(BF16),
                        ln_g[i, 2], ln_b[i, 2])
        else:
            comb = _router(X, moe_router[li])
            X = _ffn_ln(X, moe_w_gu[li].astype(BF16), moe_w_down[li].astype(BF16),
                        ln_g[i, 2], ln_b[i, 2], comb=comb)

    st = jnp.stack
    return (X[:MP].reshape(B, S, D), X[MP:].reshape(DB, DS, D),
            st(outs_p["ckv"]), st(outs_p["kpe"]), st(outs_p["kc"]), st(outs_p["vc"]),
            st(outs_p["ks"]), st(outs_p["vs"]), st(outs_p["kw"]), st(outs_p["vw"]),
            st(outs_p["mk"]), st(outs_p["mv"]),
            st(outs_s["ckv"]), st(outs_s["kpe"]), st(outs_s["kc"]), st(outs_s["vc"]),
            st(outs_s["ks"]), st(outs_s["vs"]), st(outs_s["kw"]), st(outs_s["vw"]))
```
